```python
import math
import jax
import jax.numpy as jnp
from jax import lax
import numpy as np

D_MODEL = 4096
BATCH = 2
SEQ = 8192
DEPTH = 2

GRID_W = 64
CTX_LEN = 256
EPS = 1e-6
ROPE_BASE = 10000.0
Q_BLOCK = 128

DA_HEADS = 6
DA_DK = 128
DA_DV = 2 * DA_DK
DA_WIDTH = DA_HEADS * DA_DV
MLA_HEADS = 12
MLA_Q_RANK = 768
MLA_KV_RANK = 512
MLA_NOPE = 128
MLA_ROPE = 64
MLA_DV = 128
MLA_WIDTH = MLA_HEADS * MLA_DV
CM_GROUPS = 8
CM_CHUNK = 128
CM_DG = 128
CM_WIDTH = CM_GROUPS * CM_DG
MIX_WIDTH = DA_WIDTH + MLA_WIDTH + CM_WIDTH

OFF_DA_Q = 0
OFF_DA_K = OFF_DA_Q + DA_HEADS * 2 * DA_DK
OFF_DA_V = OFF_DA_K + DA_HEADS * 2 * DA_DK
OFF_MLA_CQ = OFF_DA_V + DA_WIDTH
OFF_MLA_CKV = OFF_MLA_CQ + MLA_Q_RANK
OFF_MLA_KR = OFF_MLA_CKV + MLA_KV_RANK
OFF_CM = OFF_MLA_KR + MLA_ROPE
IN_COLS = OFF_CM + 2 * CM_WIDTH

PEER_HEADS = 8
PEER_KEYS = 128
PEER_N = PEER_KEYS * PEER_KEYS
PEER_DKEY = 256
PEER_TOPK = 16
PEER_BLOCK = 128

kernel_name = 'hybrid_diffattn_mla_chunkmlp_peer_dit'


def rmsnorm(x, g):
    xf = x.astype(jnp.float32)
    y = xf * lax.rsqrt(jnp.mean(xf * xf, axis=-1, keepdims=True) + EPS)
    return (y * g.astype(jnp.float32)).astype(x.dtype)


def modulate(x, shift, scale):
    return x * (1 + scale) + shift


def rope_axis(x, pos):
    half = x.shape[-1] // 2
    inv = ROPE_BASE ** (-jnp.arange(half, dtype=jnp.float32) / half)
    ang = pos.astype(jnp.float32)[:, None] * inv[None, :]
    cos = jnp.cos(ang)[None, :, None, :]
    sin = jnp.sin(ang)[None, :, None, :]
    xf = x.astype(jnp.float32)
    x1, x2 = xf[..., :half], xf[..., half:]
    return jnp.concatenate([x1 * cos - x2 * sin, x1 * sin + x2 * cos], axis=-1).astype(x.dtype)


def rope_2d(x, row, col):
    a = x.shape[-1] // 2
    return jnp.concatenate([rope_axis(x[..., :a], row), rope_axis(x[..., a:], col)], axis=-1)


def block_attention(qs, ks, v, coefs, scale):
    B, Sq, H, dk = qs[0].shape
    M = len(qs)
    dv = v.shape[-1]
    nb = Sq // Q_BLOCK
    qb = jnp.stack(qs, 0).reshape(M, B, nb, Q_BLOCK, H, dk).transpose(2, 0, 1, 3, 4, 5)
    kk = jnp.stack(ks, 0)
    coef = jnp.stack([jnp.asarray(cf, jnp.float32) for cf in coefs])

    def one(qblk):
        s = jnp.einsum('mbqhd,mbkhd->mbhqk', qblk, kk).astype(jnp.float32) * scale
        p = jax.nn.softmax(s, axis=-1)
        w = jnp.einsum('m,mbhqk->bhqk', coef, p)
        return jnp.einsum('bhqk,bkhd->bqhd', w.astype(v.dtype), v)

    out = lax.map(one, qb)
    return out.transpose(1, 0, 2, 3, 4).reshape(B, Sq, H, dv)


def diff_lambda(lv, lam_init):
    lv = lv.astype(jnp.float32)
    return jnp.exp(jnp.sum(lv[0] * lv[1])) - jnp.exp(jnp.sum(lv[2] * lv[3])) + lam_init


def diff_qkv(z, pos):
    B, S, _ = z.shape
    q = z[..., OFF_DA_Q:OFF_DA_K].reshape(B, S, DA_HEADS, 2, DA_DK)
    k = z[..., OFF_DA_K:OFF_DA_V].reshape(B, S, DA_HEADS, 2, DA_DK)
    v = z[..., OFF_DA_V:OFF_MLA_CQ].reshape(B, S, DA_HEADS, DA_DV)
    q1, q2, k1, k2 = q[..., 0, :], q[..., 1, :], k[..., 0, :], k[..., 1, :]
    if pos is not None:
        q1, q2, k1, k2 = (rope_2d(a, pos[0], pos[1]) for a in (q1, q2, k1, k2))
    return q1, q2, k1, k2, v


def diff_attn(q1, q2, k1, k2, v, lam, lam_init, subln_g):
    B, S = q1.shape[:2]
    o = block_attention((q1, q2), (k1, k2), v, (1.0, -lam), DA_DK ** -0.5)
    o = rmsnorm(o, subln_g) * (1.0 - lam_init)
    return o.reshape(B, S, DA_WIDTH)


def mla_qkv(z, q_norm_g, kv_norm_g, w_uq, w_ukv, pos):
    B, S, _ = z.shape
    cq = rmsnorm(z[..., OFF_MLA_CQ:OFF_MLA_CKV], q_norm_g)
    ckv = rmsnorm(z[..., OFF_MLA_CKV:OFF_MLA_KR], kv_norm_g)
    k_pe = z[..., OFF_MLA_KR:OFF_CM][:, :, None, :]
    q = (cq @ w_uq).reshape(B, S, MLA_HEADS, MLA_NOPE + MLA_ROPE)
    kv = (ckv @ w_ukv).reshape(B, S, MLA_HEADS, MLA_NOPE + MLA_DV)
    q_nope, q_pe = q[..., :MLA_NOPE], q[..., MLA_NOPE:]
    k_nope, v = kv[..., :MLA_NOPE], kv[..., MLA_NOPE:]
    if pos is not None:
        q_pe = rope_2d(q_pe, pos[0], pos[1])
        k_pe = rope_2d(k_pe, pos[0], pos[1])
    q = jnp.concatenate([q_nope, q_pe], axis=-1)
    k = jnp.concatenate([k_nope, jnp.broadcast_to(k_pe, (B, S, MLA_HEADS, MLA_ROPE))], axis=-1)
    return q, k, v


def mla_attn(q, k, v):
    B, S = q.shape[:2]
    o = block_attention((q,), (k,), v, (1.0,), (MLA_NOPE + MLA_ROPE) ** -0.5)
    return o.reshape(B, S, MLA_WIDTH)


def chunk_mlp(z_uv, v_norm_g, w_s, b_s):
    B, S, _ = z_uv.shape
    uv = jax.nn.gelu(z_uv)
    u = uv[..., :CM_WIDTH]
    v = rmsnorm(uv[..., CM_WIDTH:], v_norm_g)
    v = v.reshape(B, S // CM_CHUNK, CM_CHUNK, CM_GROUPS, CM_DG)
    v = jnp.einsum('gpq,bnqgc->bnpgc', w_s, v) + b_s.T[None, None, :, :, None]
    return u * v.reshape(B, S, CM_WIDTH)


def peer(h, w_q, sub_keys, expert_u, expert_v):
    B, S, D = h.shape
    hb_all = h.reshape((B * S) // PEER_BLOCK, PEER_BLOCK, D)

    def one(hb):
        tb = hb.shape[0]
        q = (hb @ w_q).reshape(tb, PEER_HEADS, 2, PEER_DKEY // 2)
        s = jnp.einsum('thsd,hsnd->thsn', q, sub_keys).astype(jnp.float32)
        top_s, top_i = lax.top_k(s, PEER_TOPK)
        cand = top_s[..., 0, :, None] + top_s[..., 1, None, :]
        cand_idx = top_i[..., 0, :, None] * PEER_KEYS + top_i[..., 1, None, :]
        best_s, best_pos = lax.top_k(cand.reshape(tb, PEER_HEADS, PEER_TOPK * PEER_TOPK), PEER_TOPK)
        idx = jnp.take_along_axis(cand_idx.reshape(tb, PEER_HEADS, PEER_TOPK * PEER_TOPK), best_pos, axis=-1)
        g = jax.nn.softmax(best_s, axis=-1)
        u = jnp.take(expert_u, idx, axis=0)
        a = jax.nn.gelu(jnp.einsum('thkd,td->thk', u, hb))
        vv = jnp.take(expert_v, idx, axis=0)
        return jnp.einsum('thk,thkd->td', (g * a).astype(hb.dtype), vv)

    return lax.map(one, hb_all).reshape(B, S, D)


def setup_inputs(seed: int = 0) -> dict:
    key = jax.random.key(seed)
    ks = jax.random.split(key, 24)
    f32 = jnp.float32
    L, D = DEPTH, D_MODEL

    def nrm(k, shape, s):
        return jax.random.normal(k, shape, f32) * s

    def gain(k, shape):
        return 1.0 + 0.02 * jax.random.normal(k, shape, f32)

    return {
        'x': nrm(ks[0], (BATCH, SEQ, D), 1.0),
        'c': nrm(ks[1], (BATCH, D), 1.0),
        'ctx': nrm(ks[2], (BATCH, CTX_LEN, D), 1.0),
        'c_ctx': nrm(ks[3], (D,), 1.0),
        'w_mod': nrm(ks[4], (L, D, 6 * D), 0.5 * D ** -0.5),
        'b_mod': nrm(ks[5], (L, 6 * D), 0.02),
        'norm1_g': gain(ks[6], (L, D)),
        'norm2_g': gain(ks[7], (L, D)),
        'w_in': nrm(ks[8], (L, D, IN_COLS), D ** -0.5),
        'da_lambda': nrm(ks[9], (L, 4, DA_DK), 0.1),
        'da_subln_g': gain(ks[10], (L, DA_DV)),
        'mla_q_norm_g': gain(ks[11], (L, MLA_Q_RANK)),
        'mla_kv_norm_g': gain(ks[12], (L, MLA_KV_RANK)),
        'mla_w_uq': nrm(ks[13], (L, MLA_Q_RANK, MLA_HEADS * (MLA_NOPE + MLA_ROPE)), MLA_Q_RANK ** -0.5),
        'mla_w_ukv': nrm(ks[14], (L, MLA_KV_RANK, MLA_HEADS * (MLA_NOPE + MLA_DV)), MLA_KV_RANK ** -0.5),
        'cm_v_norm_g': gain(ks[15], (L, CM_WIDTH)),
        'cm_w_s': nrm(ks[16], (L, CM_GROUPS, CM_CHUNK, CM_CHUNK), CM_CHUNK ** -0.5),
        'cm_b_s': nrm(ks[17], (L, CM_GROUPS, CM_CHUNK), 0.02),
        'w_out': nrm(ks[18], (L, MIX_WIDTH, D), MIX_WIDTH ** -0.5),
        'peer_w_q': nrm(ks[19], (L, D, PEER_HEADS * PEER_DKEY), D ** -0.5),
        'peer_sub_keys': nrm(ks[20], (L, PEER_HEADS, 2, PEER_KEYS, PEER_DKEY // 2), (PEER_DKEY // 2) ** -0.5),
        'peer_u': nrm(ks[21], (L, PEER_N, D), D ** -0.5),
        'peer_v': nrm(ks[22], (L, PEER_N, D), 1.0),
        'final_norm_g': gain(ks[23], (D,)),
    }


def reference(x, c, ctx, c_ctx, w_mod, b_mod, norm1_g, norm2_g, w_in, da_lambda, da_subln_g,
              mla_q_norm_g, mla_kv_norm_g, mla_w_uq, mla_w_ukv, cm_v_norm_g, cm_w_s, cm_b_s,
              w_out, peer_w_q, peer_sub_keys, peer_u, peer_v, final_norm_g):
    B, S, D = x.shape
    rows = S // GRID_W
    row = jnp.repeat(jnp.arange(rows, dtype=jnp.int32), GRID_W)
    col = jnp.tile(jnp.arange(GRID_W, dtype=jnp.int32), rows)
    pos = (row, col)
    silu_c = jax.nn.silu(c)
    silu_cc = jax.nn.silu(c_ctx)

    for l in range(DEPTH):
        last = l == DEPTH - 1
        mod_x = (silu_c @ w_mod[l] + b_mod[l])[:, None, :]
        mod_c = (silu_cc @ w_mod[l] + b_mod[l])[None, None, :]
        shx1, scx1, gtx1, shx2, scx2, gtx2 = jnp.split(mod_x, 6, axis=-1)
        shc1, scc1, gtc1, shc2, scc2, gtc2 = jnp.split(mod_c, 6, axis=-1)

        zx = modulate(rmsnorm(x, norm1_g[l]), shx1, scx1) @ w_in[l]
        zc = modulate(rmsnorm(ctx, norm1_g[l]), shc1, scc1) @ w_in[l]

        lam_init = 0.8 - 0.6 * math.exp(-0.3 * l)
        lam = diff_lambda(da_lambda[l], lam_init)
        q1x, q2x, k1x, k2x, vax = diff_qkv(zx, pos)
        q1c, q2c, k1c, k2c, vac = diff_qkv(zc, None)
        ax = diff_attn(q1x, q2x,
                       jnp.concatenate([k1c, k1x], axis=1), jnp.concatenate([k2c, k2x], axis=1),
                       jnp.concatenate([vac, vax], axis=1), lam, lam_init, da_subln_g[l])

        qbx, kbx, vbx = mla_qkv(zx, mla_q_norm_g[l], mla_kv_norm_g[l], mla_w_uq[l], mla_w_ukv[l], pos)
        qbc, kbc, vbc = mla_qkv(zc, mla_q_norm_g[l], mla_kv_norm_g[l], mla_w_uq[l], mla_w_ukv[l], None)
        bx = mla_attn(qbx, jnp.concatenate([kbc, kbx], axis=1), jnp.concatenate([vbc, vbx], axis=1))

        cx = chunk_mlp(zx[..., OFF_CM:], cm_v_norm_g[l], cm_w_s[l], cm_b_s[l])

        x = x + gtx1 * (jnp.concatenate([ax, bx, cx], axis=-1) @ w_out[l])
        x = x + gtx2 * peer(modulate(rmsnorm(x, norm2_g[l]), shx2, scx2),
                            peer_w_q[l], peer_sub_keys[l], peer_u[l], peer_v[l])

        if not last:
            ac = diff_attn(q1c, q2c, k1c, k2c, vac, lam, lam_init, da_subln_g[l])
            bc = mla_attn(qbc, kbc, vbc)
            cc = chunk_mlp(zc[..., OFF_CM:], cm_v_norm_g[l], cm_w_s[l], cm_b_s[l])
            ctx = ctx + gtc1 * (jnp.concatenate([ac, bc, cc], axis=-1) @ w_out[l])
            ctx = ctx + gtc2 * peer(modulate(rmsnorm(ctx, norm2_g[l]), shc2, scc2),
                                    peer_w_q[l], peer_sub_keys[l], peer_u[l], peer_v[l])

    return rmsnorm(x, final_norm_g)
```

```python
import functools
import math

import jax
import jax.numpy as jnp
from jax import lax
from jax.experimental import pallas as pl
from jax.experimental.pallas import tpu as pltpu

F32 = jnp.float32
BF16 = jnp.bfloat16
EPS = 1e-6
ROPE_BASE = 10000.0
LANES = 128
MIB = 1024 * 1024
NT_DIMS = (((1,), (1,)), ((), ()))
TN_DIMS = (((0,), (0,)), ((), ()))


class _Cfg:
    def __init__(self, d_model=4096, batch=2, seq=8192, depth=2, grid_w=64, ctx_len=256,
                 da_heads=6, mla_heads=12, mla_q_rank=768, mla_kv_rank=512,
                 cm_groups=8, peer_heads=8):
        self.D, self.B, self.S, self.L = d_model, batch, seq, depth
        self.grid_w, self.C = grid_w, ctx_len
        self.da_heads, self.da_dk, self.da_dv = da_heads, 128, 256
        self.mla_heads, self.q_rank, self.kv_rank = mla_heads, mla_q_rank, mla_kv_rank
        self.nope, self.rope, self.mla_dv = 128, 64, 128
        self.cm_groups, self.cm_chunk, self.cm_dg = cm_groups, 128, 128
        self.peer_heads, self.peer_keys, self.peer_dkey, self.topk = peer_heads, 128, 256, 16
        self.da_w = da_heads * self.da_dv
        self.mla_w = mla_heads * self.mla_dv
        self.cm_w = cm_groups * self.cm_dg
        self.mix_w = self.da_w + self.mla_w + self.cm_w
        self.off_da_k = da_heads * 2 * self.da_dk
        self.off_da_v = 2 * self.off_da_k
        self.off_cq = self.off_da_v + self.da_w
        self.off_ckv = self.off_cq + mla_q_rank
        self.off_kr = self.off_ckv + mla_kv_rank
        self.off_cm = self.off_kr + self.rope
        self.in_cols = self.off_cm + 2 * self.cm_w
        self.n_exp = self.peer_keys * self.peer_keys
        self.Tx = batch * seq
        self.Tc = batch * ctx_len
        self.T = self.Tx + self.Tc
        self.Sk = ctx_len + seq


def _pick(n, cands):
    for c in cands:
        if n % c == 0:
            return c
    raise ValueError(f"no tile in {cands} divides {n}")


def _params(sem, vmem_mib):
    return pltpu.CompilerParams(dimension_semantics=sem, vmem_limit_bytes=int(vmem_mib * MIB))


def _rms(x, g):
    return x * lax.rsqrt(jnp.mean(x * x, axis=-1, keepdims=True) + EPS) * g


def _gelu(x):
    return 0.5 * x * (1.0 + jnp.tanh(math.sqrt(2.0 / math.pi) * (x + 0.044715 * (x * x * x))))


def _mod_kernel(c_ref, w_ref, b_ref, o_ref):
    c = c_ref[...]
    s = (c * jax.nn.sigmoid(c)).astype(BF16)
    o_ref[...] = jnp.dot(s, w_ref[...].astype(BF16), preferred_element_type=F32) + b_ref[...]


def _mod_call(cfg, c_all, w_mod, b_mod):
    L, D = cfg.L, cfg.D
    n = 6 * D
    tn = _pick(n, (512, 256, 128))
    return pl.pallas_call(
        _mod_kernel,
        out_shape=jax.ShapeDtypeStruct((L, 8, n), F32),
        grid=(L, n // tn),
        in_specs=[pl.BlockSpec((8, D), lambda l, j: (0, 0)),
                  pl.BlockSpec((None, D, tn), lambda l, j: (l, 0, j)),
                  pl.BlockSpec((None, 1, tn), lambda l, j: (l, 0, j))],
        out_specs=pl.BlockSpec((None, 8, tn), lambda l, j: (l, 0, j)),
        compiler_params=_params(("arbitrary", "arbitrary"), 2 * D * tn * 4 / MIB + 3 * D * tn * 2 / MIB + 4),
        name="adaln_mod",
    )(c_all, w_mod, b_mod.reshape(L, 1, n))


def _norm_mod_kernel(x_ref, g_ref, sh_ref, sc_ref, o_ref):
    y = _rms(x_ref[...], g_ref[...])
    o_ref[...] = (y * (1.0 + sc_ref[...]) + sh_ref[...]).astype(o_ref.dtype)


def _seg_of_tile(cfg, tm):
    return lambda i: jnp.minimum((i * tm) // cfg.S, cfg.B)


def _norm_mod_call(cfg, x, g, mods, k_shift, k_scale, rows):
    D = cfg.D
    tm = _pick(math.gcd(cfg.S, cfg.Tc), (512, 256, 128))
    seg = _seg_of_tile(cfg, tm)
    return pl.pallas_call(
        _norm_mod_kernel,
        out_shape=jax.ShapeDtypeStruct((rows, D), BF16),
        grid=(rows // tm,),
        in_specs=[pl.BlockSpec((tm, D), lambda i: (i, 0)),
                  pl.BlockSpec((1, D), lambda i: (0, 0)),
                  pl.BlockSpec((None, None, 1, D), lambda i: (seg(i), k_shift, 0, 0)),
                  pl.BlockSpec((None, None, 1, D), lambda i: (seg(i), k_scale, 0, 0))],
        out_specs=pl.BlockSpec((tm, D), lambda i: (i, 0)),
        compiler_params=_params(("arbitrary",), 5 * tm * D * 4 / MIB + 4),
        name="norm_modulate",
    )(x, g.reshape(1, D), mods, mods)


def _mm_kernel(a_ref, b_ref, o_ref):
    o_ref[...] = jnp.dot(a_ref[...], b_ref[...], preferred_element_type=F32).astype(o_ref.dtype)


MM_VMEM_BUDGET_MIB = 48


def _mm_vmem_mib(tm, tn, K, osz):
    return (2 * (tm * K * 2 + K * tn * 2 + tm * tn * osz) + tm * tn * 4) / MIB + 4


def _mm_tiles(M, K, N, osz):
    for tm in (1024, 512, 256, 128):
        for tn in (N, 1536, 1024, 768, 512, 256, 128):
            if M % tm == 0 and N % tn == 0 and (tn == N or tn % LANES == 0) \
                    and _mm_vmem_mib(tm, tn, K, osz) <= MM_VMEM_BUDGET_MIB:
                return tm, tn
    raise ValueError(f"no matmul tiling for {(M, K, N)}")


def _mm_call(a, b, out_dtype, name, rows=None):
    M = a.shape[0] if rows is None else rows
    K, N = b.shape
    osz = jnp.dtype(out_dtype).itemsize
    tm, tn = _mm_tiles(M, K, N, osz)
    vm = _mm_vmem_mib(tm, tn, K, osz)
    return pl.pallas_call(
        _mm_kernel,
        out_shape=jax.ShapeDtypeStruct((M, N), out_dtype),
        grid=(M // tm, N // tn),
        in_specs=[pl.BlockSpec((tm, K), lambda i, j: (i, 0)),
                  pl.BlockSpec((K, tn), lambda i, j: (0, j))],
        out_specs=pl.BlockSpec((tm, tn), lambda i, j: (i, j)),
        compiler_params=_params(("arbitrary", "arbitrary"), vm),
        name=name,
    )(a, b)


def _mm_res_kernel(a_ref, b_ref, x_ref, g_ref, o_ref):
    acc = jnp.dot(a_ref[...], b_ref[...], preferred_element_type=F32)
    o_ref[...] = x_ref[...] + g_ref[...] * acc


def _mm_res_call(cfg, a, b, x, mods, k_gate, rows, name):
    K, N = b.shape
    tm = _pick(math.gcd(cfg.S, cfg.Tc), (512, 256, 128))
    tn = _pick(N, (1024, 512, 256, 128))
    seg = _seg_of_tile(cfg, tm)
    vm = 2 * (tm * K * 2 + K * tn * 2 + 2 * tm * tn * 4) / MIB + tm * tn * 4 / MIB + 4
    return pl.pallas_call(
        _mm_res_kernel,
        out_shape=jax.ShapeDtypeStruct((rows, N), F32),
        grid=(rows // tm, N // tn),
        in_specs=[pl.BlockSpec((tm, K), lambda i, j: (i, 0)),
                  pl.BlockSpec((K, tn), lambda i, j: (0, j)),
                  pl.BlockSpec((tm, tn), lambda i, j: (i, j)),
                  pl.BlockSpec((None, None, 1, tn), lambda i, j: (seg(i), k_gate, 0, j))],
        out_specs=pl.BlockSpec((tm, tn), lambda i, j: (i, j)),
        compiler_params=_params(("arbitrary", "arbitrary"), vm),
        name=name,
    )(a, b, x, mods)


def _key_block(cfg, tm):
    n_lat = cfg.Tx // tm
    per_b = cfg.S // tm
    per_c = cfg.C // tm

    def batch(i):
        return jnp.where(i < n_lat, i // per_b, (i - n_lat) // per_c)

    def row(i):
        return jnp.where(i < n_lat, per_c + i % per_b, (i - n_lat) % per_c)

    return batch, row


def _swap_halves(x, lane, half):
    first = (lane % (2 * half)) < half
    return jnp.where(first, pltpu.roll(x, LANES - half, 1), pltpu.roll(x, half, 1))


def _da_prep_kernel(q_ref, k_ref, v_ref, cos_ref, sin_ref, qo_ref, ko_ref, vo_ref, *, groups, qscale):
    cos = cos_ref[...]
    sin = sin_ref[...]
    lane = lax.broadcasted_iota(jnp.int32, cos.shape, 1)

    def rope(x):
        return x * cos + _swap_halves(x, lane, 32) * sin

    for g in range(groups):
        sl = slice(g * LANES, (g + 1) * LANES)
        qo_ref[:, sl] = (rope(q_ref[:, sl].astype(F32)) * qscale).astype(BF16)
        ko_ref[:, sl] = rope(k_ref[:, sl].astype(F32)).astype(BF16)
    vo_ref[...] = v_ref[...]


def _da_prep_call(cfg, z_da, cos, sin):
    W = cfg.da_w
    tm = cfg.C if cfg.C <= 256 else 256
    kb, kr = _key_block(cfg, tm)
    tok = lambda c: pl.BlockSpec((tm, W), lambda i: (i, c))
    tab = pl.BlockSpec((tm, LANES), lambda i: (i, 0))
    key = pl.BlockSpec((None, tm, W), lambda i: (kb(i), kr(i), 0))
    return pl.pallas_call(
        functools.partial(_da_prep_kernel, groups=W // LANES, qscale=cfg.da_dk ** -0.5),
        out_shape=(jax.ShapeDtypeStruct((cfg.T, W), BF16),
                   jax.ShapeDtypeStruct((cfg.B, cfg.Sk, W), BF16),
                   jax.ShapeDtypeStruct((cfg.B, cfg.Sk, W), BF16)),
        grid=(cfg.T // tm,),
        in_specs=[tok(0), tok(1), tok(2), tab, tab],
        out_specs=(tok(0), key, key),
        compiler_params=_params(("arbitrary",), 12 * tm * W * 2 / MIB + 8),
        name="da_prep",
    )(z_da, z_da, z_da, cos, sin)


def _mla_prep_kernel(z_ref, gq_ref, gkv_ref, wq_ref, wkv_ref, cos_ref, sin_ref,
                     qo_ref, ko_ref, vo_ref, *, heads, q_rank, kv_rank, qscale):
    z = z_ref[...].astype(F32)
    cos = cos_ref[...]
    sin = sin_ref[...]
    lane = lax.broadcasted_iota(jnp.int32, cos.shape, 1)

    def rope(x):
        return x * cos + _swap_halves(x, lane, 16) * sin

    cq = _rms(z[:, :q_rank], gq_ref[...]).astype(BF16)
    ckv = _rms(z[:, q_rank:q_rank + kv_rank], gkv_ref[...]).astype(BF16)
    kpe = rope(z[:, q_rank + kv_rank:q_rank + kv_rank + LANES]).astype(BF16)
    q = jnp.dot(cq, wq_ref[...], preferred_element_type=F32)
    kv = jnp.dot(ckv, wkv_ref[...], preferred_element_type=F32)
    for h in range(heads):
        a, b, c = 2 * h * LANES, (2 * h + 1) * LANES, (2 * h + 2) * LANES
        qo_ref[:, a:b] = (q[:, a:b] * qscale).astype(BF16)
        qo_ref[:, b:c] = (rope(q[:, b:c]) * qscale).astype(BF16)
        ko_ref[:, a:b] = kv[:, a:b].astype(BF16)
        ko_ref[:, b:c] = kpe
        vo_ref[:, h * LANES:(h + 1) * LANES] = kv[:, b:c].astype(BF16)


def _mla_prep_call(cfg, z_mla, gq, gkv, wq, wkv, cos, sin):
    H = cfg.mla_heads
    Wz = z_mla.shape[1]
    tm = cfg.C if cfg.C <= 256 else 256
    kb, kr = _key_block(cfg, tm)
    full = lambda a: pl.BlockSpec(a.shape, lambda i: (0,) * a.ndim)
    tab = pl.BlockSpec((tm, LANES), lambda i: (i, 0))
    return pl.pallas_call(
        functools.partial(_mla_prep_kernel, heads=H, q_rank=cfg.q_rank, kv_rank=cfg.kv_rank,
                          qscale=(cfg.nope + cfg.rope) ** -0.5),
        out_shape=(jax.ShapeDtypeStruct((cfg.T, H * 256), BF16),
                   jax.ShapeDtypeStruct((cfg.B, cfg.Sk, H * 256), BF16),
                   jax.ShapeDtypeStruct((cfg.B, cfg.Sk, H * LANES), BF16)),
        grid=(cfg.T // tm,),
        in_specs=[pl.BlockSpec((tm, Wz), lambda i: (i, 0)), full(gq), full(gkv), full(wq), full(wkv), tab, tab],
        out_specs=(pl.BlockSpec((tm, H * 256), lambda i: (i, 0)),
                   pl.BlockSpec((None, tm, H * 256), lambda i: (kb(i), kr(i), 0)),
                   pl.BlockSpec((None, tm, H * LANES), lambda i: (kb(i), kr(i), 0))),
        compiler_params=_params(("arbitrary",), 2 * (wq.size + wkv.size) * 2 / MIB + 24 * tm * H * 256 / MIB + 8),
        name="mla_prep",
    )(z_mla, gq, gkv, wq, wkv, cos, sin)


def _online_step(q, k, v, m, l, acc):
    s = lax.dot_general(q, k, NT_DIMS, preferred_element_type=F32)
    m_new = jnp.maximum(m, jnp.max(s, axis=-1, keepdims=True))
    alpha = jnp.exp(m - m_new)
    p = jnp.exp(s - m_new)
    l = alpha * l + jnp.sum(p, axis=-1, keepdims=True)
    acc = alpha * acc + jnp.dot(p.astype(BF16), v, preferred_element_type=F32)
    return m_new, l, acc


def _da_attn_kernel(lam_ref, g_ref, q_ref, k_ref, v_ref, o_ref, *, tk, nk, lam_init):
    tq = q_ref.shape[0]
    q1 = q_ref[:, :LANES]
    q2 = q_ref[:, LANES:]

    def body(kk, carry):
        m1, l1, a1, m2, l2, a2 = carry
        off = pl.multiple_of(kk * tk, tk)
        k = k_ref[pl.ds(off, tk), :]
        v = v_ref[pl.ds(off, tk), :]
        m1, l1, a1 = _online_step(q1, k[:, :LANES], v, m1, l1, a1)
        m2, l2, a2 = _online_step(q2, k[:, LANES:], v, m2, l2, a2)
        return m1, l1, a1, m2, l2, a2

    neg = jnp.full((tq, 1), -jnp.inf, F32)
    zero = jnp.zeros((tq, 1), F32)
    acc0 = jnp.zeros((tq, v_ref.shape[1]), F32)
    m1, l1, a1, m2, l2, a2 = lax.fori_loop(0, nk, body, (neg, zero, acc0, neg, zero, acc0))
    lv = lam_ref[...]
    lam = (jnp.exp(jnp.sum(lv[0:1] * lv[1:2], axis=-1, keepdims=True))
           - jnp.exp(jnp.sum(lv[2:3] * lv[3:4], axis=-1, keepdims=True)) + lam_init)
    o = a1 / l1 - lam * (a2 / l2)
    o_ref[...] = (_rms(o, g_ref[...]) * (1.0 - lam_init)).astype(o_ref.dtype)


def _da_attn_call(cfg, q, k, v, lam_vec, subln_g, lam_init, *, q_row0, sq, sk, out_rows):
    H, dv = cfg.da_heads, cfg.da_dv
    tq = _pick(sq, (256, 128))
    tk = _pick(sk, (768, 512, 256, 128))
    nq = sq // tq
    q0 = q_row0 // tq
    vm = 2 * (2 * sk * 256 * 2) / MIB + 6 * tq * tk * 4 / MIB + 16
    return pl.pallas_call(
        functools.partial(_da_attn_kernel, tk=tk, nk=sk // tk, lam_init=lam_init),
        out_shape=jax.ShapeDtypeStruct((out_rows, H * dv), BF16),
        grid=(cfg.B, H, nq),
        in_specs=[pl.BlockSpec((4, LANES), lambda b, h, i: (0, 0)),
                  pl.BlockSpec((1, dv), lambda b, h, i: (0, 0)),
                  pl.BlockSpec((tq, 256), lambda b, h, i: (q0 + b * nq + i, h)),
                  pl.BlockSpec((None, sk, 256), lambda b, h, i: (b, 0, h)),
                  pl.BlockSpec((None, sk, dv), lambda b, h, i: (b, 0, h))],
        out_specs=pl.BlockSpec((tq, dv), lambda b, h, i: (b * nq + i, h)),
        compiler_params=_params(("arbitrary", "arbitrary", "arbitrary"), vm),
        name="diff_attention",
    )(lam_vec, subln_g.reshape(1, dv), q, k, v)


def _mla_attn_kernel(q_ref, k_ref, v_ref, o_ref, *, tk, nk):
    tq = q_ref.shape[0]
    q = q_ref[...]

    def body(kk, carry):
        off = pl.multiple_of(kk * tk, tk)
        return _online_step(q, k_ref[pl.ds(off, tk), :], v_ref[pl.ds(off, tk), :], *carry)

    init = (jnp.full((tq, 1), -jnp.inf, F32), jnp.zeros((tq, 1), F32), jnp.zeros((tq, v_ref.shape[1]), F32))
    _, l, acc = lax.fori_loop(0, nk, body, init)
    o_ref[...] = (acc / l).astype(o_ref.dtype)


def _mla_attn_call(cfg, q, k, v, *, q_row0, sq, sk, out_rows):
    H, dv = cfg.mla_heads, cfg.mla_dv
    tq = _pick(sq, (256, 128))
    tk = _pick(sk, (768, 512, 256, 128))
    nq = sq // tq
    q0 = q_row0 // tq
    vm = 2 * (sk * 384 * 2) / MIB + 6 * tq * tk * 4 / MIB + 16
    return pl.pallas_call(
        functools.partial(_mla_attn_kernel, tk=tk, nk=sk // tk),
        out_shape=jax.ShapeDtypeStruct((out_rows, H * dv), BF16),
        grid=(cfg.B, H, nq),
        in_specs=[pl.BlockSpec((tq, 256), lambda b, h, i: (q0 + b * nq + i, h)),
                  pl.BlockSpec((None, sk, 256), lambda b, h, i: (b, 0, h)),
                  pl.BlockSpec((None, sk, dv), lambda b, h, i: (b, 0, h))],
        out_specs=pl.BlockSpec((tq, dv), lambda b, h, i: (b * nq + i, h)),
        compiler_params=_params(("arbitrary", "arbitrary", "arbitrary"), vm),
        name="mla_attention",
    )(q, k, v)


def _chunk_mlp_kernel(z_ref, g_ref, ws_ref, b_ref, o_ref, *, groups, chunk):
    uv = _gelu(z_ref[...].astype(F32))
    w = groups * LANES
    u = uv[:, :w]
    v = _rms(uv[:, w:], g_ref[...]).astype(BF16)
    for c in range(z_ref.shape[0] // chunk):
        rows = slice(c * chunk, (c + 1) * chunk)
        for g in range(groups):
            cols = slice(g * LANES, (g + 1) * LANES)
            y = jnp.dot(ws_ref[g], v[rows, cols], preferred_element_type=F32) + b_ref[:, cols]
            o_ref[rows, cols] = (u[rows, cols] * y).astype(o_ref.dtype)


def _chunk_mlp_call(cfg, z_cm, g, ws, b_full, rows):
    W = cfg.cm_w
    tm = _pick(math.gcd(cfg.S, cfg.Tc), (256, 128))
    return pl.pallas_call(
        functools.partial(_chunk_mlp_kernel, groups=cfg.cm_groups, chunk=cfg.cm_chunk),
        out_shape=jax.ShapeDtypeStruct((rows, W), BF16),
        grid=(rows // tm,),
        in_specs=[pl.BlockSpec((tm, 2 * W), lambda i: (i, 0)),
                  pl.BlockSpec((1, W), lambda i: (0, 0)),
                  pl.BlockSpec(ws.shape, lambda i: (0, 0, 0)),
                  pl.BlockSpec(b_full.shape, lambda i: (0, 0))],
        out_specs=pl.BlockSpec((tm, W), lambda i: (i, 0)),
        compiler_params=_params(("arbitrary",), 40 * tm * W / MIB + 8),
        name="chunk_mlp",
    )(z_cm, g.reshape(1, W), ws, b_full)


def _top_k_rows(sc, topk):
    n, tm = sc.shape
    iota = lax.broadcasted_iota(jnp.int32, (n, tm), 0)
    iota_k = lax.broadcasted_iota(jnp.int32, (topk, tm), 0)

    def body(k, carry):
        work, rank, vals = carry
        m = jnp.max(work, axis=0, keepdims=True)
        idx = jnp.min(jnp.where(work == m, iota, n), axis=0, keepdims=True)
        hit = iota == idx
        return (jnp.where(hit, -jnp.inf, work), jnp.where(hit, k, rank), jnp.where(iota_k == k, m, vals))

    init = (sc, jnp.full((n, tm), topk, jnp.int32), jnp.zeros((topk, tm), F32))
    _, rank, vals = lax.fori_loop(0, topk, body, init)
    return rank, vals


def _route_kernel(q_ref, keys_ref, er_ref, ec_ref, rm_ref, cb_ref, *, topk):
    half = q_ref.shape[1] // 2
    s_row = lax.dot_general(keys_ref[0], q_ref[:, :half], NT_DIMS, preferred_element_type=F32)
    s_col = lax.dot_general(keys_ref[1], q_ref[:, half:], NT_DIMS, preferred_element_type=F32)
    rank_r, top_r = _top_k_rows(s_row, topk)
    rank_c, top_c = _top_k_rows(s_col, topk)
    cand = jnp.concatenate([top_r[a:a + 1, :] + top_c for a in range(topk)], axis=0)
    rank_p, best = _top_k_rows(cand, topk)
    chosen = rank_p < topk
    z = jnp.sum(jnp.where(chosen, jnp.exp(cand - best[0:1, :]), 0.0), axis=0, keepdims=True)
    iota_p = lax.broadcasted_iota(jnp.int32, cand.shape, 0)
    bits = jnp.where(chosen, jnp.left_shift(1, iota_p % topk), 0).astype(F32)
    rm = jnp.zeros(s_row.shape, jnp.int32)
    for a in range(topk):
        row_bits = jnp.sum(bits[a * topk:(a + 1) * topk, :], axis=0, keepdims=True).astype(jnp.int32)
        rm = jnp.where(rank_r == a, row_bits, rm)
    er_ref[...] = jnp.where(rank_r < topk, jnp.exp(s_row - top_r[0:1, :]), 0.0)
    ec_ref[...] = jnp.where(rank_c < topk, jnp.exp(s_col - top_c[0:1, :]) / z, 0.0)
    rm_ref[...] = rm
    cb_ref[...] = jnp.where(rank_c < topk, jnp.left_shift(1, rank_c), 0)


def _route_call(cfg, qp, keys, rows):
    H, nk = cfg.peer_heads, cfg.peer_keys
    tm = _pick(rows, (512, 256, 128))
    out = lambda dt: jax.ShapeDtypeStruct((H, nk, rows), dt)
    ospec = pl.BlockSpec((None, nk, tm), lambda i, h: (h, 0, i))
    return pl.pallas_call(
        functools.partial(_route_kernel, topk=cfg.topk),
        out_shape=(out(F32), out(F32), out(jnp.int32), out(jnp.int32)),
        grid=(rows // tm, H),
        in_specs=[pl.BlockSpec((tm, cfg.peer_dkey), lambda i, h: (i, h)),
                  pl.BlockSpec((None, 2, nk, cfg.peer_dkey // 2), lambda i, h: (h, 0, 0, 0))],
        out_specs=(ospec, ospec, ospec, ospec),
        compiler_params=_params(("arbitrary", "arbitrary"), 32),
        name="peer_route",
    )(qp, keys)


def _peer_act_kernel(u_ref, h_ref, er_ref, rm_ref, ec_ref, cb_ref, o_ref, *, heads):
    act = _gelu(lax.dot_general(u_ref[...], h_ref[...], NT_DIMS, preferred_element_type=F32))
    for r in range(u_ref.shape[0] // LANES):
        rows = slice(r * LANES, (r + 1) * LANES)
        w = jnp.zeros((LANES, h_ref.shape[0]), F32)
        for h in range(heads):
            hit = (cb_ref[h] & rm_ref[h, r:r + 1, :]) != 0
            w = w + jnp.where(hit, ec_ref[h] * er_ref[h, r:r + 1, :], 0.0)
        o_ref[rows, :] = (w * act[rows, :]).astype(o_ref.dtype)


def _peer_act_call(cfg, u, h2, er, ec, rm, cb, rows):
    H, nk, D = cfg.peer_heads, cfg.peer_keys, cfg.D
    tn = 1024
    tm = _pick(rows, (512, 256, 128))
    R = tn // LANES
    row_spec = pl.BlockSpec((H, R, tm), lambda i, n: (0, n, i))
    col_spec = pl.BlockSpec((H, nk, tm), lambda i, n: (0, 0, i))
    vm = 2 * (tn * D * 2 + tm * D * 2 + tn * tm * 2 + 2 * H * nk * tm * 4) / MIB + 4 * tn * tm * 4 / MIB + 6
    return pl.pallas_call(
        functools.partial(_peer_act_kernel, heads=H),
        out_shape=jax.ShapeDtypeStruct((cfg.n_exp, rows), BF16),
        grid=(rows // tm, cfg.n_exp // tn),
        in_specs=[pl.BlockSpec((tn, D), lambda i, n: (n, 0)),
                  pl.BlockSpec((tm, D), lambda i, n: (i, 0)),
                  row_spec, row_spec, col_spec, col_spec],
        out_specs=pl.BlockSpec((tn, tm), lambda i, n: (n, i)),
        compiler_params=_params(("arbitrary", "arbitrary"), vm),
        name="peer_activations",
    )(u, h2, er, rm, ec, cb)


def _peer_out_kernel(pt_ref, v_ref, x_ref, g_ref, o_ref, acc_ref):
    k = pl.program_id(2)

    @pl.when(k == 0)
    def _():
        acc_ref[...] = jnp.zeros_like(acc_ref)

    acc_ref[...] += lax.dot_general(pt_ref[...], v_ref[...], TN_DIMS, preferred_element_type=F32)

    @pl.when(k == pl.num_programs(2) - 1)
    def _():
        o_ref[...] = x_ref[...] + g_ref[...] * acc_ref[...]


def _peer_out_call(cfg, pt, v, x, mods, k_gate, rows):
    D = cfg.D
    tm = _pick(math.gcd(cfg.S, cfg.Tc), (512, 256, 128))
    tn = _pick(D, (1024, 512, 256, 128))
    tk = _pick(cfg.n_exp, (2048, 1024))
    seg = _seg_of_tile(cfg, tm)
    vm = 2 * (tk * tm * 2 + tk * tn * 2 + 2 * tm * tn * 4) / MIB + 3 * tm * tn * 4 / MIB + 6
    return pl.pallas_call(
        _peer_out_kernel,
        out_shape=jax.ShapeDtypeStruct((rows, D), F32),
        grid=(rows // tm, D // tn, cfg.n_exp // tk),
        in_specs=[pl.BlockSpec((tk, tm), lambda i, j, k: (k, i)),
                  pl.BlockSpec((tk, tn), lambda i, j, k: (k, j)),
                  pl.BlockSpec((tm, tn), lambda i, j, k: (i, j)),
                  pl.BlockSpec((None, None, 1, tn), lambda i, j, k: (seg(i), k_gate, 0, j))],
        out_specs=pl.BlockSpec((tm, tn), lambda i, j, k: (i, j)),
        scratch_shapes=[pltpu.VMEM((tm, tn), F32)],
        compiler_params=_params(("arbitrary", "arbitrary", "arbitrary"), vm),
        name="peer_output",
    )(pt, v, x, mods)


def _final_norm_kernel(x_ref, g_ref, o_ref):
    o_ref[...] = _rms(x_ref[...], g_ref[...])


def _final_norm_call(cfg, x, g):
    D = cfg.D
    tm = _pick(cfg.Tx, (512, 256, 128))
    return pl.pallas_call(
        _final_norm_kernel,
        out_shape=jax.ShapeDtypeStruct((cfg.Tx, D), F32),
        grid=(cfg.Tx // tm,),
        in_specs=[pl.BlockSpec((tm, D), lambda i: (i, 0)), pl.BlockSpec((1, D), lambda i: (0, 0))],
        out_specs=pl.BlockSpec((tm, D), lambda i: (i, 0)),
        compiler_params=_params(("arbitrary",), 6 * tm * D * 4 / MIB + 4),
        name="final_norm",
    )(x, g.reshape(1, D))


def _rope_tables(cfg, half, pad):
    pos = jnp.arange(cfg.S, dtype=jnp.int32)
    inv = ROPE_BASE ** (-jnp.arange(half, dtype=F32) / half)
    ang_r = (pos // cfg.grid_w).astype(F32)[:, None] * inv[None, :]
    ang_c = (pos % cfg.grid_w).astype(F32)[:, None] * inv[None, :]
    cos = jnp.concatenate([jnp.cos(ang_r)] * 2 + [jnp.cos(ang_c)] * 2, axis=-1)
    sin = jnp.concatenate([-jnp.sin(ang_r), jnp.sin(ang_r), -jnp.sin(ang_c), jnp.sin(ang_c)], axis=-1)
    cos = jnp.concatenate([jnp.tile(cos, (cfg.B, 1)), jnp.ones((cfg.Tc, 4 * half), F32)], axis=0)
    sin = jnp.concatenate([jnp.tile(sin, (cfg.B, 1)), jnp.zeros((cfg.Tc, 4 * half), F32)], axis=0)
    if pad:
        cos = jnp.pad(cos, ((0, 0), (0, pad)))
        sin = jnp.pad(sin, ((0, 0), (0, pad)))
    return cos, sin


def _forward(cfg, x, c, ctx, c_ctx, w_mod, b_mod, norm1_g, norm2_g, w_in, da_lambda, da_subln_g,
             mla_q_norm_g, mla_kv_norm_g, mla_w_uq, mla_w_ukv, cm_v_norm_g, cm_w_s, cm_b_s,
             w_out, peer_w_q, peer_sub_keys, peer_u, peer_v, final_norm_g):
    B, S, C, D, L = cfg.B, cfg.S, cfg.C, cfg.D, cfg.L
    assert cfg.da_dk == LANES and cfg.nope == LANES and cfg.peer_keys == LANES and cfg.topk == 16
    assert C % LANES == 0 and S % C == 0 and cfg.q_rank % LANES == 0 and cfg.kv_rank % LANES == 0

    xs = jnp.concatenate([x.reshape(B * S, D), ctx.reshape(B * C, D)], axis=0)
    c_all = jnp.concatenate([c, c_ctx[None, :], jnp.zeros((8 - B - 1, D), F32)], axis=0)
    mods_all = _mod_call(cfg, c_all, w_mod, b_mod).reshape(L, 8, 6, 1, D)
    cos_da, sin_da = _rope_tables(cfg, 32, 0)
    cos_ml, sin_ml = _rope_tables(cfg, 16, LANES - cfg.rope)

    for l in range(L):
        last = l == L - 1
        rows = cfg.Tx if last else cfg.T
        mods = mods_all[l]
        lam_init = 0.8 - 0.6 * math.exp(-0.3 * l)

        w_in_l = w_in[l]
        w_da = w_in_l[:, :cfg.off_cq].astype(BF16)
        ml_cols = cfg.off_cm - cfg.off_cq
        w_ml = jnp.pad(w_in_l[:, cfg.off_cq:cfg.off_cm], ((0, 0), (0, -ml_cols % 256))).astype(BF16)
        w_cm = w_in_l[:, cfg.off_cm:].astype(BF16)
        H = cfg.mla_heads
        wq = jnp.pad(mla_w_uq[l].reshape(cfg.q_rank, H, cfg.nope + cfg.rope),
                     ((0, 0), (0, 0), (0, LANES - cfg.rope))).reshape(cfg.q_rank, H * 256).astype(BF16)
        wkv = mla_w_ukv[l].astype(BF16)
        b_full = jnp.repeat(cm_b_s[l].T, cfg.cm_dg, axis=1)
        keys = peer_sub_keys[l].astype(BF16)

        h1 = _norm_mod_call(cfg, xs, norm1_g[l], mods, 0, 1, cfg.T)
        z_da = _mm_call(h1, w_da, BF16, "in_proj_da")
        z_ml = _mm_call(h1, w_ml, BF16, "in_proj_mla")
        z_cm = _mm_call(h1, w_cm, BF16, "in_proj_cm", rows=rows)
        q_da, k_da, v_da = _da_prep_call(cfg, z_da, cos_da, sin_da)
        q_ml, k_ml, v_ml = _mla_prep_call(cfg, z_ml, mla_q_norm_g[l].reshape(1, -1), mla_kv_norm_g[l].reshape(1, -1),
                                          wq, wkv, cos_ml, sin_ml)
        parts_a = [_da_attn_call(cfg, q_da, k_da, v_da, da_lambda[l], da_subln_g[l], lam_init,
                                 q_row0=0, sq=S, sk=cfg.Sk, out_rows=cfg.Tx)]
        parts_b = [_mla_attn_call(cfg, q_ml, k_ml, v_ml, q_row0=0, sq=S, sk=cfg.Sk, out_rows=cfg.Tx)]
        if not last:
            parts_a.append(_da_attn_call(cfg, q_da, k_da, v_da, da_lambda[l], da_subln_g[l], lam_init,
                                         q_row0=cfg.Tx, sq=C, sk=C, out_rows=cfg.Tc))
            parts_b.append(_mla_attn_call(cfg, q_ml, k_ml, v_ml, q_row0=cfg.Tx, sq=C, sk=C, out_rows=cfg.Tc))
        mix_c = _chunk_mlp_call(cfg, z_cm, cm_v_norm_g[l], cm_w_s[l].astype(BF16), b_full, rows)
        mix = jnp.concatenate([jnp.concatenate(parts_a, axis=0), jnp.concatenate(parts_b, axis=0), mix_c], axis=1)
        xs = _mm_res_call(cfg, mix, w_out[l].astype(BF16), xs, mods, 2, rows, "out_proj")

        h2 = _norm_mod_call(cfg, xs, norm2_g[l], mods, 3, 4, rows)
        qp = _mm_call(h2, peer_w_q[l].astype(BF16), BF16, "peer_query")
        er, ec, rm, cb = _route_call(cfg, qp, keys, rows)
        pt = _peer_act_call(cfg, peer_u[l].astype(BF16), h2, er, ec, rm, cb, rows)
        xs = _peer_out_call(cfg, pt, peer_v[l].astype(BF16), xs, mods, 5, rows)

    return _final_norm_call(cfg, xs, final_norm_g).reshape(B, S, D)


def kernel(x, c, ctx, c_ctx, w_mod, b_mod, norm1_g, norm2_g, w_in, da_lambda, da_subln_g, mla_q_norm_g,
           mla_kv_norm_g, mla_w_uq, mla_w_ukv, cm_v_norm_g, cm_w_s, cm_b_s, w_out, peer_w_q, peer_sub_keys,
           peer_u, peer_v, final_norm_g):
    return _forward(_Cfg(), x, c, ctx, c_ctx, w_mod, b_mod, norm1_g, norm2_g, w_in, da_lambda, da_subln_g,
                    mla_q_norm_g, mla_kv_norm_g, mla_w_uq, mla_w_ukv, cm_v_norm_g, cm_w_s, cm_b_s, w_out,
                    peer_w_q, peer_sub_keys, peer_u, peer_v, final_norm_g)
```

```python
import functools
import math

import jax
import jax.numpy as jnp
from jax import lax
from jax.experimental import pallas as pl
from jax.experimental.pallas import tpu as pltpu

F32 = jnp.float32
BF16 = jnp.bfloat16
EPS = 1e-6
ROPE_BASE = 10000.0
LANES = 128
MIB = 1024 * 1024
LOG2E = math.log2(math.e)
NT_DIMS = (((1,), (1,)), ((), ()))
TN_DIMS = (((0,), (0,)), ((), ()))


class _Cfg:
    def __init__(self, d_model=4096, batch=2, seq=8192, depth=2, grid_w=64, ctx_len=256,
                 da_heads=6, mla_heads=12, mla_q_rank=768, mla_kv_rank=512,
                 cm_groups=8, peer_heads=8):
        self.D, self.B, self.S, self.L = d_model, batch, seq, depth
        self.grid_w, self.C = grid_w, ctx_len
        self.da_heads, self.da_dk, self.da_dv = da_heads, 128, 256
        self.mla_heads, self.q_rank, self.kv_rank = mla_heads, mla_q_rank, mla_kv_rank
        self.nope, self.rope, self.mla_dv = 128, 64, 128
        self.cm_groups, self.cm_chunk, self.cm_dg = cm_groups, 128, 128
        self.peer_heads, self.peer_keys, self.peer_dkey, self.topk = peer_heads, 128, 256, 16
        self.da_w = da_heads * self.da_dv
        self.mla_w = mla_heads * self.mla_dv
        self.cm_w = cm_groups * self.cm_dg
        self.mix_w = self.da_w + self.mla_w + self.cm_w
        self.off_da_k = da_heads * 2 * self.da_dk
        self.off_da_v = 2 * self.off_da_k
        self.off_cq = self.off_da_v + self.da_w
        self.off_ckv = self.off_cq + mla_q_rank
        self.off_kr = self.off_ckv + mla_kv_rank
        self.off_cm = self.off_kr + self.rope
        self.in_cols = self.off_cm + 2 * self.cm_w
        self.n_exp = self.peer_keys * self.peer_keys
        self.Tx = batch * seq
        self.Tc = batch * ctx_len
        self.T = self.Tx + self.Tc
        self.Sk = ctx_len + seq


def _pick(n, cands):
    for c in cands:
        if n % c == 0:
            return c
    raise ValueError(f"no tile in {cands} divides {n}")


def _params(sem, vmem_mib, flags=None):
    return pltpu.CompilerParams(dimension_semantics=sem, vmem_limit_bytes=int(vmem_mib * MIB), flags=flags)


def _rms(x, g):
    return x * lax.rsqrt(jnp.mean(x * x, axis=-1, keepdims=True) + EPS) * g


def _gelu(x):
    return 0.5 * x * (1.0 + jnp.tanh(math.sqrt(2.0 / math.pi) * (x + 0.044715 * (x * x * x))))


def _mod_kernel(c_ref, w_ref, b_ref, o_ref):
    c = c_ref[...]
    s = (c * jax.nn.sigmoid(c)).astype(BF16)
    o_ref[...] = jnp.dot(s, w_ref[...].astype(BF16), preferred_element_type=F32) + b_ref[...]


def _mod_call(cfg, c_all, w_mod, b_mod):
    L, D = cfg.L, cfg.D
    n = 6 * D
    tn = _pick(n, (512, 256, 128))
    return pl.pallas_call(
        _mod_kernel,
        out_shape=jax.ShapeDtypeStruct((L, 8, n), F32),
        grid=(L, n // tn),
        in_specs=[pl.BlockSpec((8, D), lambda l, j: (0, 0)),
                  pl.BlockSpec((None, D, tn), lambda l, j: (l, 0, j)),
                  pl.BlockSpec((None, 1, tn), lambda l, j: (l, 0, j))],
        out_specs=pl.BlockSpec((None, 8, tn), lambda l, j: (l, 0, j)),
        compiler_params=_params(("arbitrary", "arbitrary"), 2 * D * tn * 4 / MIB + 3 * D * tn * 2 / MIB + 4),
        name="adaln_mod",
    )(c_all, w_mod, b_mod.reshape(L, 1, n))


def _norm_mod_kernel(x_ref, g_ref, sh_ref, sc_ref, o_ref):
    y = _rms(x_ref[...], g_ref[...])
    o_ref[...] = (y * (1.0 + sc_ref[...]) + sh_ref[...]).astype(o_ref.dtype)


def _seg_of_tile(cfg, tm):
    return lambda i: jnp.minimum((i * tm) // cfg.S, cfg.B)


def _norm_mod_call(cfg, x, g, mods, k_shift, k_scale, rows):
    D = cfg.D
    tm = _pick(math.gcd(cfg.S, cfg.Tc), (512, 256, 128))
    seg = _seg_of_tile(cfg, tm)
    return pl.pallas_call(
        _norm_mod_kernel,
        out_shape=jax.ShapeDtypeStruct((rows, D), BF16),
        grid=(rows // tm,),
        in_specs=[pl.BlockSpec((tm, D), lambda i: (i, 0)),
                  pl.BlockSpec((1, D), lambda i: (0, 0)),
                  pl.BlockSpec((None, None, 1, D), lambda i: (seg(i), k_shift, 0, 0)),
                  pl.BlockSpec((None, None, 1, D), lambda i: (seg(i), k_scale, 0, 0))],
        out_specs=pl.BlockSpec((tm, D), lambda i: (i, 0)),
        compiler_params=_params(("arbitrary",), 5 * tm * D * 4 / MIB + 4),
        name="norm_modulate",
    )(x, g.reshape(1, D), mods, mods)


def _mm_kernel(a_ref, b_ref, o_ref):
    o_ref[...] = jnp.dot(a_ref[...], b_ref[...], preferred_element_type=F32).astype(o_ref.dtype)


MM_VMEM_BUDGET_MIB = 48


def _mm_vmem_mib(tm, tn, K, osz):
    return (2 * (tm * K * 2 + K * tn * 2 + tm * tn * osz) + tm * tn * 4) / MIB + 4


def _mm_tiles(M, K, N, osz):
    for tm in (1024, 512, 256, 128):
        for tn in (N, 1536, 1024, 768, 512, 256, 128):
            if M % tm == 0 and N % tn == 0 and (tn == N or tn % LANES == 0) \
                    and _mm_vmem_mib(tm, tn, K, osz) <= MM_VMEM_BUDGET_MIB:
                return tm, tn
    raise ValueError(f"no matmul tiling for {(M, K, N)}")


def _mm_call(a, b, out_dtype, name, rows=None):
    M = a.shape[0] if rows is None else rows
    K, N = b.shape
    osz = jnp.dtype(out_dtype).itemsize
    tm, tn = _mm_tiles(M, K, N, osz)
    vm = _mm_vmem_mib(tm, tn, K, osz)
    return pl.pallas_call(
        _mm_kernel,
        out_shape=jax.ShapeDtypeStruct((M, N), out_dtype),
        grid=(M // tm, N // tn),
        in_specs=[pl.BlockSpec((tm, K), lambda i, j: (i, 0)),
                  pl.BlockSpec((K, tn), lambda i, j: (0, j))],
        out_specs=pl.BlockSpec((tm, tn), lambda i, j: (i, j)),
        compiler_params=_params(("arbitrary", "arbitrary"), vm),
        name=name,
    )(a, b)


def _mm_res_kernel(a_ref, b_ref, x_ref, g_ref, o_ref):
    acc = jnp.dot(a_ref[...], b_ref[...], preferred_element_type=F32)
    o_ref[...] = x_ref[...] + g_ref[...] * acc


def _mm_res_call(cfg, a, b, x, mods, k_gate, rows, name):
    K, N = b.shape
    tm = _pick(math.gcd(cfg.S, cfg.Tc), (512, 256, 128))
    tn = _pick(N, (1024, 512, 256, 128))
    seg = _seg_of_tile(cfg, tm)
    vm = 2 * (tm * K * 2 + K * tn * 2 + 2 * tm * tn * 4) / MIB + tm * tn * 4 / MIB + 4
    return pl.pallas_call(
        _mm_res_kernel,
        out_shape=jax.ShapeDtypeStruct((rows, N), F32),
        grid=(rows // tm, N // tn),
        in_specs=[pl.BlockSpec((tm, K), lambda i, j: (i, 0)),
                  pl.BlockSpec((K, tn), lambda i, j: (0, j)),
                  pl.BlockSpec((tm, tn), lambda i, j: (i, j)),
                  pl.BlockSpec((None, None, 1, tn), lambda i, j: (seg(i), k_gate, 0, j))],
        out_specs=pl.BlockSpec((tm, tn), lambda i, j: (i, j)),
        compiler_params=_params(("arbitrary", "arbitrary"), vm),
        name=name,
    )(a, b, x, mods)


def _key_block(cfg, tm):
    n_lat = cfg.Tx // tm
    per_b = cfg.S // tm
    per_c = cfg.C // tm

    def batch(i):
        return jnp.where(i < n_lat, i // per_b, (i - n_lat) // per_c)

    def row(i):
        return jnp.where(i < n_lat, per_c + i % per_b, (i - n_lat) % per_c)

    return batch, row


def _swap_halves(x, lane, half):
    first = (lane % (2 * half)) < half
    return jnp.where(first, pltpu.roll(x, LANES - half, 1), pltpu.roll(x, half, 1))


def _da_prep_kernel(q_ref, k_ref, v_ref, cos_ref, sin_ref, qo_ref, ko_ref, vo_ref, *, groups, qscale):
    cos = cos_ref[...]
    sin = sin_ref[...]
    lane = lax.broadcasted_iota(jnp.int32, cos.shape, 1)

    def rope(x):
        return x * cos + _swap_halves(x, lane, 32) * sin

    for g in range(groups):
        sl = slice(g * LANES, (g + 1) * LANES)
        qo_ref[:, sl] = (rope(q_ref[:, sl].astype(F32)) * qscale).astype(BF16)
        ko_ref[:, sl] = rope(k_ref[:, sl].astype(F32)).astype(BF16)
    vo_ref[...] = v_ref[...]


def _da_prep_call(cfg, z_da, cos, sin):
    W = cfg.da_w
    tm = cfg.C if cfg.C <= 256 else 256
    kb, kr = _key_block(cfg, tm)
    tok = lambda c: pl.BlockSpec((tm, W), lambda i: (i, c))
    tab = pl.BlockSpec((tm, LANES), lambda i: (i, 0))
    key = pl.BlockSpec((None, tm, W), lambda i: (kb(i), kr(i), 0))
    return pl.pallas_call(
        functools.partial(_da_prep_kernel, groups=W // LANES, qscale=cfg.da_dk ** -0.5 * LOG2E),
        out_shape=(jax.ShapeDtypeStruct((cfg.T, W), BF16),
                   jax.ShapeDtypeStruct((cfg.B, cfg.Sk, W), BF16),
                   jax.ShapeDtypeStruct((cfg.B, cfg.Sk, W), BF16)),
        grid=(cfg.T // tm,),
        in_specs=[tok(0), tok(1), tok(2), tab, tab],
        out_specs=(tok(0), key, key),
        compiler_params=_params(("arbitrary",), 12 * tm * W * 2 / MIB + 8),
        name="da_prep",
    )(z_da, z_da, z_da, cos, sin)


def _mla_prep_kernel(z_ref, gq_ref, gkv_ref, wq_ref, wkv_ref, cos_ref, sin_ref,
                     qo_ref, ko_ref, vo_ref, *, heads, q_rank, kv_rank, qscale):
    z = z_ref[...].astype(F32)
    cos = cos_ref[...]
    sin = sin_ref[...]
    lane = lax.broadcasted_iota(jnp.int32, cos.shape, 1)

    def rope(x):
        return x * cos + _swap_halves(x, lane, 16) * sin

    cq = _rms(z[:, :q_rank], gq_ref[...]).astype(BF16)
    ckv = _rms(z[:, q_rank:q_rank + kv_rank], gkv_ref[...]).astype(BF16)
    kpe = rope(z[:, q_rank + kv_rank:q_rank + kv_rank + LANES]).astype(BF16)
    q = jnp.dot(cq, wq_ref[...], preferred_element_type=F32)
    kv = jnp.dot(ckv, wkv_ref[...], preferred_element_type=F32)
    for h in range(heads):
        a, b, c = 2 * h * LANES, (2 * h + 1) * LANES, (2 * h + 2) * LANES
        qo_ref[:, a:b] = (q[:, a:b] * qscale).astype(BF16)
        qo_ref[:, b:c] = (rope(q[:, b:c]) * qscale).astype(BF16)
        ko_ref[:, a:b] = kv[:, a:b].astype(BF16)
        ko_ref[:, b:c] = kpe
        vo_ref[:, a:b] = kv[:, b:c].astype(BF16)
        vo_ref[:, b:c] = jnp.ones((z.shape[0], LANES), BF16)


def _mla_prep_call(cfg, z_mla, gq, gkv, wq, wkv, cos, sin):
    H = cfg.mla_heads
    Wz = z_mla.shape[1]
    tm = cfg.C if cfg.C <= 256 else 256
    kb, kr = _key_block(cfg, tm)
    full = lambda a: pl.BlockSpec(a.shape, lambda i: (0,) * a.ndim)
    tab = pl.BlockSpec((tm, LANES), lambda i: (i, 0))
    return pl.pallas_call(
        functools.partial(_mla_prep_kernel, heads=H, q_rank=cfg.q_rank, kv_rank=cfg.kv_rank,
                          qscale=(cfg.nope + cfg.rope) ** -0.5 * LOG2E),
        out_shape=(jax.ShapeDtypeStruct((cfg.T, H * 256), BF16),
                   jax.ShapeDtypeStruct((cfg.B, cfg.Sk, H * 256), BF16),
                   jax.ShapeDtypeStruct((cfg.B, cfg.Sk, H * 256), BF16)),
        grid=(cfg.T // tm,),
        in_specs=[pl.BlockSpec((tm, Wz), lambda i: (i, 0)), full(gq), full(gkv), full(wq), full(wkv), tab, tab],
        out_specs=(pl.BlockSpec((tm, H * 256), lambda i: (i, 0)),
                   pl.BlockSpec((None, tm, H * 256), lambda i: (kb(i), kr(i), 0)),
                   pl.BlockSpec((None, tm, H * 256), lambda i: (kb(i), kr(i), 0))),
        compiler_params=_params(("arbitrary",), 2 * (wq.size + wkv.size) * 2 / MIB + 24 * tm * H * 256 / MIB + 8),
        name="mla_prep",
    )(z_mla, gq, gkv, wq, wkv, cos, sin)


def _online_step(q, k, v, m, l, acc):
    s = lax.dot_general(q, k, NT_DIMS, preferred_element_type=F32)
    m_new = jnp.maximum(m, jnp.max(s, axis=-1, keepdims=True))
    alpha = jnp.exp2(m - m_new)
    p = jnp.exp2(s - m_new)
    l = alpha * l + jnp.sum(p, axis=-1, keepdims=True)
    acc = alpha * acc + jnp.dot(p.astype(BF16), v, preferred_element_type=F32)
    return m_new, l, acc


def _da_attn_kernel(lam_ref, g_ref, q_ref, k_ref, v_ref, o_ref, *, tk, nk, lam_init):
    tq = q_ref.shape[0]
    q1 = q_ref[:, :LANES]
    q2 = q_ref[:, LANES:]

    neg = jnp.full((tq, 1), -jnp.inf, F32)
    zero = jnp.zeros((tq, 1), F32)
    acc0 = jnp.zeros((tq, v_ref.shape[1]), F32)
    m1, l1, a1, m2, l2, a2 = neg, zero, acc0, neg, zero, acc0
    for kk in range(nk):
        rows = slice(kk * tk, (kk + 1) * tk)
        v = v_ref[rows, :]
        m1, l1, a1 = _online_step(q1, k_ref[rows, :LANES], v, m1, l1, a1)
        m2, l2, a2 = _online_step(q2, k_ref[rows, LANES:], v, m2, l2, a2)
    lv = lam_ref[...]
    lam = (jnp.exp(jnp.sum(lv[0:1] * lv[1:2], axis=-1, keepdims=True))
           - jnp.exp(jnp.sum(lv[2:3] * lv[3:4], axis=-1, keepdims=True)) + lam_init)
    o = a1 / l1 - lam * (a2 / l2)
    o_ref[...] = (_rms(o, g_ref[...]) * (1.0 - lam_init)).astype(o_ref.dtype)


def _da_attn_call(cfg, q, k, v, lam_vec, subln_g, lam_init, *, q_row0, sq, sk, out_rows):
    H, dv = cfg.da_heads, cfg.da_dv
    tq = _pick(sq, (512, 256, 128))
    tk = _pick(sk, (768, 512, 256, 128))
    nq = sq // tq
    q0 = q_row0 // tq
    vm = 2 * (2 * sk * 256 * 2) / MIB + 16 * tq * tk * 4 / MIB + 12
    return pl.pallas_call(
        functools.partial(_da_attn_kernel, tk=tk, nk=sk // tk, lam_init=lam_init),
        out_shape=jax.ShapeDtypeStruct((out_rows, H * dv), BF16),
        grid=(cfg.B, H, nq),
        in_specs=[pl.BlockSpec((4, LANES), lambda b, h, i: (0, 0)),
                  pl.BlockSpec((1, dv), lambda b, h, i: (0, 0)),
                  pl.BlockSpec((tq, 256), lambda b, h, i: (q0 + b * nq + i, h)),
                  pl.BlockSpec((None, sk, 256), lambda b, h, i: (b, 0, h)),
                  pl.BlockSpec((None, sk, dv), lambda b, h, i: (b, 0, h))],
        out_specs=pl.BlockSpec((tq, dv), lambda b, h, i: (b * nq + i, h)),
        compiler_params=_params(("arbitrary", "arbitrary", "arbitrary"), vm),
        name="diff_attention",
    )(lam_vec, subln_g.reshape(1, dv), q, k, v)


def _mla_attn_kernel(q_ref, k_ref, v_ref, o_ref, *, tk, nk):
    tq = q_ref.shape[0]
    q = q_ref[...]

    m, acc = jnp.full((tq, 1), -jnp.inf, F32), jnp.zeros((tq, v_ref.shape[1]), F32)
    for kk in range(nk):
        rows = slice(kk * tk, (kk + 1) * tk)
        s = lax.dot_general(q, k_ref[rows, :], NT_DIMS, preferred_element_type=F32)
        m_new = jnp.maximum(m, jnp.max(s, axis=-1, keepdims=True))
        p = jnp.exp2(s - m_new).astype(BF16)
        acc = jnp.exp2(m - m_new) * acc + jnp.dot(p, v_ref[rows, :], preferred_element_type=F32)
        m = m_new
    dv = o_ref.shape[1]
    o_ref[...] = (acc[:, :dv] / acc[:, dv:dv + 1]).astype(o_ref.dtype)


def _mla_attn_call(cfg, q, k, v, *, q_row0, sq, sk, out_rows):
    H, dv = cfg.mla_heads, cfg.mla_dv
    tq = _pick(sq, (512, 256, 128))
    tk = _pick(sk, (768, 512, 256, 128))
    nq = sq // tq
    q0 = q_row0 // tq
    vm = 2 * (sk * 512 * 2) / MIB + 6 * tq * tk * 4 / MIB + 16
    return pl.pallas_call(
        functools.partial(_mla_attn_kernel, tk=tk, nk=sk // tk),
        out_shape=jax.ShapeDtypeStruct((out_rows, H * dv), BF16),
        grid=(cfg.B, H, nq),
        in_specs=[pl.BlockSpec((tq, 256), lambda b, h, i: (q0 + b * nq + i, h)),
                  pl.BlockSpec((None, sk, 256), lambda b, h, i: (b, 0, h)),
                  pl.BlockSpec((None, sk, 2 * dv), lambda b, h, i: (b, 0, h))],
        out_specs=pl.BlockSpec((tq, dv), lambda b, h, i: (b * nq + i, h)),
        compiler_params=_params(("arbitrary", "arbitrary", "arbitrary"), vm),
        name="mla_attention",
    )(q, k, v)


def _chunk_mlp_kernel(z_ref, g_ref, ws_ref, b_ref, o_ref, *, groups, chunk):
    uv = _gelu(z_ref[...].astype(F32))
    w = groups * LANES
    u = uv[:, :w]
    v = _rms(uv[:, w:], g_ref[...]).astype(BF16)
    for c in range(z_ref.shape[0] // chunk):
        rows = slice(c * chunk, (c + 1) * chunk)
        for g in range(groups):
            cols = slice(g * LANES, (g + 1) * LANES)
            y = jnp.dot(ws_ref[g], v[rows, cols], preferred_element_type=F32) + b_ref[:, cols]
            o_ref[rows, cols] = (u[rows, cols] * y).astype(o_ref.dtype)


def _chunk_mlp_call(cfg, z_cm, g, ws, b_full, rows):
    W = cfg.cm_w
    tm = _pick(math.gcd(cfg.S, cfg.Tc), (256, 128))
    return pl.pallas_call(
        functools.partial(_chunk_mlp_kernel, groups=cfg.cm_groups, chunk=cfg.cm_chunk),
        out_shape=jax.ShapeDtypeStruct((rows, W), BF16),
        grid=(rows // tm,),
        in_specs=[pl.BlockSpec((tm, 2 * W), lambda i: (i, 0)),
                  pl.BlockSpec((1, W), lambda i: (0, 0)),
                  pl.BlockSpec(ws.shape, lambda i: (0, 0, 0)),
                  pl.BlockSpec(b_full.shape, lambda i: (0, 0))],
        out_specs=pl.BlockSpec((tm, W), lambda i: (i, 0)),
        compiler_params=_params(("arbitrary",), 40 * tm * W / MIB + 8),
        name="chunk_mlp",
    )(z_cm, g.reshape(1, W), ws, b_full)


INT32_MIN = -2 ** 31


def _order_key(x):
    b = lax.bitcast_convert_type(x, jnp.int32)
    return jnp.where(b < 0, b ^ 0x7FFFFFFF, b)


def _order_key_inv(k):
    return lax.bitcast_convert_type(jnp.where(k < 0, k ^ 0x7FFFFFFF, k), F32)


def _top_k_sublanes(keys, pos, topk):
    iota_k = lax.broadcasted_iota(jnp.int32, (topk, keys[0].shape[1]), 0)

    def body(k, carry):
        out = []
        for (work, vals), p in zip(carry, pos):
            m = jnp.max(work, axis=0, keepdims=True)
            idx = jnp.min(jnp.where(work == m, p, jnp.int32(2 ** 30)), axis=0, keepdims=True)
            out.append((jnp.where(p == idx, INT32_MIN + k, work), jnp.where(iota_k == k, m, vals)))
        return tuple(out)

    init = tuple((w, jnp.zeros((topk, w.shape[1]), jnp.int32)) for w in keys)
    res = lax.fori_loop(0, topk, body, init)
    return [(jnp.where(w < INT32_MIN + topk, w - INT32_MIN, topk), v) for w, v in res]


def _pair_blocks(topk):
    return [(0, topk)] + [(a, 8) for a in range(1, 8)]


def _route_kernel(q_ref, keys_ref, er_ref, cnt_ref, ec_ref, rk_ref, *, topk):
    half = q_ref.shape[1] // 2
    tm = q_ref.shape[0]
    s_row = lax.dot_general(keys_ref[0], q_ref[:, :half], NT_DIMS, preferred_element_type=F32)
    s_col = lax.dot_general(keys_ref[1], q_ref[:, half:], NT_DIMS, preferred_element_type=F32)
    iota = lax.broadcasted_iota(jnp.int32, (s_row.shape[0], LANES), 0)
    ranks_r, ranks_c, tops_r, tops_c = [], [], [], []
    for g in range(tm // LANES):
        lanes = slice(g * LANES, (g + 1) * LANES)
        (rr, vr), (rc, vc) = _top_k_sublanes([_order_key(s_row[:, lanes]), _order_key(s_col[:, lanes])],
                                             [iota, iota], topk)
        ranks_r.append(rr)
        ranks_c.append(rc)
        tops_r.append(_order_key_inv(vr))
        tops_c.append(_order_key_inv(vc))
    rank_r = jnp.concatenate(ranks_r, axis=1)
    rank_c = jnp.concatenate(ranks_c, axis=1)
    top_r = jnp.concatenate(tops_r, axis=1)
    top_c = jnp.concatenate(tops_c, axis=1)

    blocks = _pair_blocks(topk)
    cand_parts, pos_parts, valid_parts = [], [], []
    for a, nb in blocks:
        b_iota = lax.broadcasted_iota(jnp.int32, (nb, tm), 0)
        cand_parts.append(top_r[a:a + 1, :] + top_c[:nb, :])
        pos_parts.append(a * topk + b_iota)
        valid_parts.append((a + 1) * (b_iota + 1) <= topk)
    a_iota = lax.broadcasted_iota(jnp.int32, (topk - 8, tm), 0) + 8
    cand_parts.append(top_r[8:, :] + top_c[0:1, :])
    pos_parts.append(a_iota * topk)
    valid_parts.append(a_iota < topk)
    cand = jnp.concatenate(cand_parts, axis=0)
    pos = jnp.concatenate(pos_parts, axis=0)
    valid = jnp.concatenate(valid_parts, axis=0)
    cand_key = jnp.where(valid, _order_key(cand), INT32_MIN + topk)
    (rank_p, best), = _top_k_sublanes([cand_key], [pos], topk)
    chosen = rank_p < topk
    best0 = _order_key_inv(best[0:1, :])
    z = jnp.sum(jnp.where(chosen, jnp.exp(cand - best0), 0.0), axis=0, keepdims=True)
    chosen_f = chosen.astype(F32)
    cnt = jnp.zeros(s_row.shape, F32)
    row0 = 0
    for a, nb in blocks:
        cnt_a = jnp.sum(chosen_f[row0:row0 + nb, :], axis=0, keepdims=True)
        cnt = jnp.where(rank_r == a, cnt_a, cnt)
        row0 += nb
    for a in range(8, topk):
        cnt = jnp.where(rank_r == a, chosen_f[row0 + a - 8:row0 + a - 7, :], cnt)
    er_ref[...] = jnp.where(rank_r < topk, jnp.exp(s_row - top_r[0:1, :]), 0.0)
    cnt_ref[...] = cnt
    ec_ref[...] = jnp.where(rank_c < topk, jnp.exp(s_col - top_c[0:1, :]) / z, 0.0)
    rk_ref[...] = rank_c.astype(F32)


def _route_call(cfg, qp, keys, rows):
    H, nk = cfg.peer_heads, cfg.peer_keys
    tm = _pick(rows, (256, 128))
    out = lambda dt: jax.ShapeDtypeStruct((H, nk, rows), dt)
    ospec = pl.BlockSpec((None, nk, tm), lambda i, h: (h, 0, i))
    return pl.pallas_call(
        functools.partial(_route_kernel, topk=cfg.topk),
        out_shape=(out(F32), out(F32), out(F32), out(F32)),
        grid=(rows // tm, H),
        in_specs=[pl.BlockSpec((tm, cfg.peer_dkey), lambda i, h: (i, h)),
                  pl.BlockSpec((None, 2, nk, cfg.peer_dkey // 2), lambda i, h: (h, 0, 0, 0))],
        out_specs=(ospec, ospec, ospec, ospec),
        compiler_params=_params(("arbitrary", "arbitrary"), 32),
        name="peer_route",
    )(qp, keys)


MM_ROW_CHUNK = 256


def _peer_act_kernel(u_ref, h_ref, er_ref, cnt_ref, ec_ref, rk_ref, o_ref, *, heads):
    per_chunk = MM_ROW_CHUNK // LANES
    for c in range(u_ref.shape[0] // MM_ROW_CHUNK):
        crow = slice(c * MM_ROW_CHUNK, (c + 1) * MM_ROW_CHUNK)
        act = _gelu(lax.dot_general(u_ref[crow, :], h_ref[...], NT_DIMS, preferred_element_type=F32))
        for rr in range(per_chunk):
            r = c * per_chunk + rr
            rows = slice(r * LANES, (r + 1) * LANES)
            for g in range(h_ref.shape[0] // LANES):
                lanes = slice(g * LANES, (g + 1) * LANES)
                w = jnp.zeros((LANES, LANES), F32)
                for h in range(heads):
                    hit = rk_ref[h, :, lanes] < cnt_ref[h, r:r + 1, lanes]
                    w = w + jnp.where(hit, ec_ref[h, :, lanes], 0.0) * er_ref[h, r:r + 1, lanes]
                o_ref[rows, lanes] = (w * act[rr * LANES:(rr + 1) * LANES, lanes]).astype(o_ref.dtype)


def _peer_act_call(cfg, u, h2, er, cnt, ec, rk, rows):
    H, nk, D = cfg.peer_heads, cfg.peer_keys, cfg.D
    tn = 1024
    tm = _pick(rows, (512, 256, 128))
    R = tn // LANES
    row_spec = pl.BlockSpec((H, R, tm), lambda i, n: (0, n, i))
    col_spec = pl.BlockSpec((H, nk, tm), lambda i, n: (0, 0, i))
    vm = 2 * (tn * D * 2 + tm * D * 2 + tn * tm * 2 + 2 * H * nk * tm * 4) / MIB + 4 * tn * tm * 4 / MIB + 6
    return pl.pallas_call(
        functools.partial(_peer_act_kernel, heads=H),
        out_shape=jax.ShapeDtypeStruct((cfg.n_exp, rows), BF16),
        grid=(rows // tm, cfg.n_exp // tn),
        in_specs=[pl.BlockSpec((tn, D), lambda i, n: (n, 0)),
                  pl.BlockSpec((tm, D), lambda i, n: (i, 0)),
                  row_spec, row_spec, col_spec, col_spec],
        out_specs=pl.BlockSpec((tn, tm), lambda i, n: (n, i)),
        compiler_params=_params(("arbitrary", "arbitrary"), vm),
        name="peer_activations",
    )(u, h2, er, cnt, ec, rk)


def _peer_out_kernel(pt_ref, v_ref, x_ref, g_ref, o_ref, acc_ref):
    k = pl.program_id(2)

    @pl.when(k == 0)
    def _():
        acc_ref[...] = jnp.zeros_like(acc_ref)

    acc_ref[...] += lax.dot_general(pt_ref[...], v_ref[...], TN_DIMS, preferred_element_type=F32)

    @pl.when(k == pl.num_programs(2) - 1)
    def _():
        o_ref[...] = x_ref[...] + g_ref[...] * acc_ref[...]


def _peer_out_call(cfg, pt, v, x, mods, k_gate, rows):
    D = cfg.D
    tm = _pick(math.gcd(cfg.S, cfg.Tc), (512, 256, 128))
    tn = _pick(D, (1024, 512, 256, 128))
    tk = _pick(cfg.n_exp, (4096, 2048, 1024))
    seg = _seg_of_tile(cfg, tm)
    vm = 2 * (tk * tm * 2 + tk * tn * 2 + 2 * tm * tn * 4) / MIB + 3 * tm * tn * 4 / MIB + 6
    return pl.pallas_call(
        _peer_out_kernel,
        out_shape=jax.ShapeDtypeStruct((rows, D), F32),
        grid=(rows // tm, D // tn, cfg.n_exp // tk),
        in_specs=[pl.BlockSpec((tk, tm), lambda i, j, k: (k, i)),
                  pl.BlockSpec((tk, tn), lambda i, j, k: (k, j)),
                  pl.BlockSpec((tm, tn), lambda i, j, k: (i, j)),
                  pl.BlockSpec((None, None, 1, tn), lambda i, j, k: (seg(i), k_gate, 0, j))],
        out_specs=pl.BlockSpec((tm, tn), lambda i, j, k: (i, j)),
        scratch_shapes=[pltpu.VMEM((tm, tn), F32)],
        compiler_params=_params(("arbitrary", "arbitrary", "arbitrary"), vm),
        name="peer_output",
    )(pt, v, x, mods)


def _final_norm_kernel(x_ref, g_ref, o_ref):
    o_ref[...] = _rms(x_ref[...], g_ref[...])


def _final_norm_call(cfg, x, g):
    D = cfg.D
    tm = _pick(cfg.Tx, (512, 256, 128))
    return pl.pallas_call(
        _final_norm_kernel,
        out_shape=jax.ShapeDtypeStruct((cfg.Tx, D), F32),
        grid=(cfg.Tx // tm,),
        in_specs=[pl.BlockSpec((tm, D), lambda i: (i, 0)), pl.BlockSpec((1, D), lambda i: (0, 0))],
        out_specs=pl.BlockSpec((tm, D), lambda i: (i, 0)),
        compiler_params=_params(("arbitrary",), 6 * tm * D * 4 / MIB + 4),
        name="final_norm",
    )(x, g.reshape(1, D))


def _rope_tables(cfg, half, pad):
    pos = jnp.arange(cfg.S, dtype=jnp.int32)
    inv = ROPE_BASE ** (-jnp.arange(half, dtype=F32) / half)
    ang_r = (pos // cfg.grid_w).astype(F32)[:, None] * inv[None, :]
    ang_c = (pos % cfg.grid_w).astype(F32)[:, None] * inv[None, :]
    cos = jnp.concatenate([jnp.cos(ang_r)] * 2 + [jnp.cos(ang_c)] * 2, axis=-1)
    sin = jnp.concatenate([-jnp.sin(ang_r), jnp.sin(ang_r), -jnp.sin(ang_c), jnp.sin(ang_c)], axis=-1)
    cos = jnp.concatenate([jnp.tile(cos, (cfg.B, 1)), jnp.ones((cfg.Tc, 4 * half), F32)], axis=0)
    sin = jnp.concatenate([jnp.tile(sin, (cfg.B, 1)), jnp.zeros((cfg.Tc, 4 * half), F32)], axis=0)
    if pad:
        cos = jnp.pad(cos, ((0, 0), (0, pad)))
        sin = jnp.pad(sin, ((0, 0), (0, pad)))
    return cos, sin


def _forward(cfg, x, c, ctx, c_ctx, w_mod, b_mod, norm1_g, norm2_g, w_in, da_lambda, da_subln_g,
             mla_q_norm_g, mla_kv_norm_g, mla_w_uq, mla_w_ukv, cm_v_norm_g, cm_w_s, cm_b_s,
             w_out, peer_w_q, peer_sub_keys, peer_u, peer_v, final_norm_g):
    B, S, C, D, L = cfg.B, cfg.S, cfg.C, cfg.D, cfg.L
    assert cfg.da_dk == LANES and cfg.nope == LANES and cfg.peer_keys == LANES and cfg.topk == 16
    assert C % LANES == 0 and S % C == 0 and cfg.q_rank % LANES == 0 and cfg.kv_rank % LANES == 0

    xs = jnp.concatenate([x.reshape(B * S, D), ctx.reshape(B * C, D)], axis=0)
    c_all = jnp.concatenate([c, c_ctx[None, :], jnp.zeros((8 - B - 1, D), F32)], axis=0)
    mods_all = _mod_call(cfg, c_all, w_mod, b_mod).reshape(L, 8, 6, 1, D)
    cos_da, sin_da = _rope_tables(cfg, 32, 0)
    cos_ml, sin_ml = _rope_tables(cfg, 16, LANES - cfg.rope)

    for l in range(L):
        last = l == L - 1
        rows = cfg.Tx if last else cfg.T
        mods = mods_all[l]
        lam_init = 0.8 - 0.6 * math.exp(-0.3 * l)

        w_in_l = w_in[l]
        w_da = w_in_l[:, :cfg.off_cq].astype(BF16)
        ml_cols = cfg.off_cm - cfg.off_cq
        w_ml = jnp.pad(w_in_l[:, cfg.off_cq:cfg.off_cm], ((0, 0), (0, -ml_cols % 256))).astype(BF16)
        w_cm = w_in_l[:, cfg.off_cm:].astype(BF16)
        H = cfg.mla_heads
        wq = jnp.pad(mla_w_uq[l].reshape(cfg.q_rank, H, cfg.nope + cfg.rope),
                     ((0, 0), (0, 0), (0, LANES - cfg.rope))).reshape(cfg.q_rank, H * 256).astype(BF16)
        wkv = mla_w_ukv[l].astype(BF16)
        b_full = jnp.repeat(cm_b_s[l].T, cfg.cm_dg, axis=1)
        keys = peer_sub_keys[l].astype(BF16)

        h1 = _norm_mod_call(cfg, xs, norm1_g[l], mods, 0, 1, cfg.T)
        z_da = _mm_call(h1, w_da, BF16, "in_proj_da")
        z_ml = _mm_call(h1, w_ml, BF16, "in_proj_mla")
        z_cm = _mm_call(h1, w_cm, BF16, "in_proj_cm", rows=rows)
        q_da, k_da, v_da = _da_prep_call(cfg, z_da, cos_da, sin_da)
        q_ml, k_ml, v_ml = _mla_prep_call(cfg, z_ml, mla_q_norm_g[l].reshape(1, -1), mla_kv_norm_g[l].reshape(1, -1),
                                          wq, wkv, cos_ml, sin_ml)
        parts_a = [_da_attn_call(cfg, q_da, k_da, v_da, da_lambda[l], da_subln_g[l], lam_init,
                                 q_row0=0, sq=S, sk=cfg.Sk, out_rows=cfg.Tx)]
        parts_b = [_mla_attn_call(cfg, q_ml, k_ml, v_ml, q_row0=0, sq=S, sk=cfg.Sk, out_rows=cfg.Tx)]
        if not last:
            parts_a.append(_da_attn_call(cfg, q_da, k_da, v_da, da_lambda[l], da_subln_g[l], lam_init,
                                         q_row0=cfg.Tx, sq=C, sk=C, out_rows=cfg.Tc))
            parts_b.append(_mla_attn_call(cfg, q_ml, k_ml, v_ml, q_row0=cfg.Tx, sq=C, sk=C, out_rows=cfg.Tc))
        mix_c = _chunk_mlp_call(cfg, z_cm, cm_v_norm_g[l], cm_w_s[l].astype(BF16), b_full, rows)
        mix = jnp.concatenate([jnp.concatenate(parts_a, axis=0), jnp.concatenate(parts_b, axis=0), mix_c], axis=1)
        xs = _mm_res_call(cfg, mix, w_out[l].astype(BF16), xs, mods, 2, rows, "out_proj")

        h2 = _norm_mod_call(cfg, xs, norm2_g[l], mods, 3, 4, rows)
        qp = _mm_call(h2, peer_w_q[l].astype(BF16), BF16, "peer_query")
        er, cnt, ec, rk = _route_call(cfg, qp, keys, rows)
        pt = _peer_act_call(cfg, peer_u[l].astype(BF16), h2, er, cnt, ec, rk, rows)
        xs = _peer_out_call(cfg, pt, peer_v[l].astype(BF16), xs, mods, 5, rows)

    return _final_norm_call(cfg, xs, final_norm_g).reshape(B, S, D)


def kernel(x, c, ctx, c_ctx, w_mod, b_mod, norm1_g, norm2_g, w_in, da_lambda, da_subln_g, mla_q_norm_g,
           mla_kv_norm_g, mla_w_uq, mla_w_ukv, cm_v_norm_g, cm_w_s, cm_b_s, w_out, peer_w_q, peer_sub_keys,
           peer_u, peer_v, final_norm_g):
    return _forward(_Cfg(), x, c, ctx, c_ctx, w_mod, b_mod, norm1_g, norm2_g, w_in, da_lambda, da_subln_g,
                    mla_q_norm_g, mla_kv_norm_g, mla_w_uq, mla_w_ukv, cm_v_norm_g, cm_w_s, cm_b_s, w_out,
                    peer_w_q, peer_sub_keys, peer_u, peer_v, final_norm_g)
```

```python
import functools
import math

import jax
import jax.numpy as jnp
from jax import lax
from jax.experimental import pallas as pl
from jax.experimental.pallas import tpu as pltpu

F32 = jnp.float32
BF16 = jnp.bfloat16
EPS = 1e-6
ROPE_BASE = 10000.0
LANES = 128
MIB = 1024 * 1024
LOG2E = math.log2(math.e)
NT_DIMS = (((1,), (1,)), ((), ()))
TN_DIMS = (((0,), (0,)), ((), ()))


class _Cfg:
    def __init__(self, d_model=4096, batch=2, seq=8192, depth=2, grid_w=64, ctx_len=256,
                 da_heads=6, mla_heads=12, mla_q_rank=768, mla_kv_rank=512,
                 cm_groups=8, peer_heads=8):
        self.D, self.B, self.S, self.L = d_model, batch, seq, depth
        self.grid_w, self.C = grid_w, ctx_len
        self.da_heads, self.da_dk, self.da_dv = da_heads, 128, 256
        self.mla_heads, self.q_rank, self.kv_rank = mla_heads, mla_q_rank, mla_kv_rank
        self.nope, self.rope, self.mla_dv = 128, 64, 128
        self.cm_groups, self.cm_chunk, self.cm_dg = cm_groups, 128, 128
        self.peer_heads, self.peer_keys, self.peer_dkey, self.topk = peer_heads, 128, 256, 16
        self.da_w = da_heads * self.da_dv
        self.mla_w = mla_heads * self.mla_dv
        self.cm_w = cm_groups * self.cm_dg
        self.mix_w = self.da_w + self.mla_w + self.cm_w
        self.off_da_k = da_heads * 2 * self.da_dk
        self.off_da_v = 2 * self.off_da_k
        self.off_cq = self.off_da_v + self.da_w
        self.off_ckv = self.off_cq + mla_q_rank
        self.off_kr = self.off_ckv + mla_kv_rank
        self.off_cm = self.off_kr + self.rope
        self.in_cols = self.off_cm + 2 * self.cm_w
        self.n_exp = self.peer_keys * self.peer_keys
        self.Tx = batch * seq
        self.Tc = batch * ctx_len
        self.T = self.Tx + self.Tc
        self.Sk = ctx_len + seq


def _pick(n, cands):
    for c in cands:
        if n % c == 0:
            return c
    raise ValueError(f"no tile in {cands} divides {n}")


def _params(sem, vmem_mib, flags=None):
    return pltpu.CompilerParams(dimension_semantics=sem, vmem_limit_bytes=int(vmem_mib * MIB), flags=flags)


def _rms(x, g):
    return x * lax.rsqrt(jnp.mean(x * x, axis=-1, keepdims=True) + EPS) * g


def _gelu(x):
    return 0.5 * x * (1.0 + jnp.tanh(math.sqrt(2.0 / math.pi) * (x + 0.044715 * (x * x * x))))


def _mod_kernel(c_ref, w_ref, b_ref, o_ref):
    c = c_ref[...]
    s = (c * jax.nn.sigmoid(c)).astype(BF16)
    o_ref[...] = jnp.dot(s, w_ref[...].astype(BF16), preferred_element_type=F32) + b_ref[...]


def _mod_call(cfg, c_all, w_mod, b_mod):
    L, D = cfg.L, cfg.D
    n = 6 * D
    tn = _pick(n, (512, 256, 128))
    return pl.pallas_call(
        _mod_kernel,
        out_shape=jax.ShapeDtypeStruct((L, 8, n), F32),
        grid=(L, n // tn),
        in_specs=[pl.BlockSpec((8, D), lambda l, j: (0, 0)),
                  pl.BlockSpec((None, D, tn), lambda l, j: (l, 0, j)),
                  pl.BlockSpec((None, 1, tn), lambda l, j: (l, 0, j))],
        out_specs=pl.BlockSpec((None, 8, tn), lambda l, j: (l, 0, j)),
        compiler_params=_params(("arbitrary", "arbitrary"), 2 * D * tn * 4 / MIB + 3 * D * tn * 2 / MIB + 4),
        name="adaln_mod",
    )(c_all, w_mod, b_mod.reshape(L, 1, n))


def _cast_kernel(w_ref, o_ref):
    o_ref[...] = w_ref[...].astype(o_ref.dtype)


def _cast_call(w, l):
    _, R, Cw = w.shape
    tr = _pick(R, (512, 256, 128))
    return pl.pallas_call(
        _cast_kernel,
        out_shape=jax.ShapeDtypeStruct((R, Cw), BF16),
        grid=(R // tr,),
        in_specs=[pl.BlockSpec((None, tr, Cw), lambda i: (l, i, 0))],
        out_specs=pl.BlockSpec((tr, Cw), lambda i: (i, 0)),
        compiler_params=_params(("arbitrary",), 2 * tr * Cw * 6 / MIB + 4),
        name="weight_cast",
    )(w)


def _norm_mod_kernel(x_ref, g_ref, sh_ref, sc_ref, o_ref):
    y = _rms(x_ref[...], g_ref[...])
    o_ref[...] = (y * (1.0 + sc_ref[...]) + sh_ref[...]).astype(o_ref.dtype)


def _seg_of_tile(cfg, tm):
    return lambda i: jnp.minimum((i * tm) // cfg.S, cfg.B)


def _norm_mod_call(cfg, x, g, mods, k_shift, k_scale, rows):
    D = cfg.D
    tm = _pick(math.gcd(cfg.S, cfg.Tc), (512, 256, 128))
    seg = _seg_of_tile(cfg, tm)
    return pl.pallas_call(
        _norm_mod_kernel,
        out_shape=jax.ShapeDtypeStruct((rows, D), BF16),
        grid=(rows // tm,),
        in_specs=[pl.BlockSpec((tm, D), lambda i: (i, 0)),
                  pl.BlockSpec((1, D), lambda i: (0, 0)),
                  pl.BlockSpec((None, None, 1, D), lambda i: (seg(i), k_shift, 0, 0)),
                  pl.BlockSpec((None, None, 1, D), lambda i: (seg(i), k_scale, 0, 0))],
        out_specs=pl.BlockSpec((tm, D), lambda i: (i, 0)),
        compiler_params=_params(("arbitrary",), 5 * tm * D * 4 / MIB + 4),
        name="norm_modulate",
    )(x, g.reshape(1, D), mods, mods)


def _mm_kernel(a_ref, b_ref, o_ref):
    o_ref[...] = jnp.dot(a_ref[...], b_ref[...], preferred_element_type=F32).astype(o_ref.dtype)


MM_VMEM_BUDGET_MIB = 48


def _mm_vmem_mib(tm, tn, K, osz):
    return (2 * (tm * K * 2 + K * tn * 2 + tm * tn * osz) + tm * tn * 4) / MIB + 4


def _mm_tiles(M, K, N, osz):
    for tm in (1024, 512, 256, 128):
        for tn in (N, 1536, 1024, 768, 512, 256, 128):
            if M % tm == 0 and N % tn == 0 and (tn == N or tn % LANES == 0) \
                    and _mm_vmem_mib(tm, tn, K, osz) <= MM_VMEM_BUDGET_MIB:
                return tm, tn
    raise ValueError(f"no matmul tiling for {(M, K, N)}")


def _mm_call(a, b, out_dtype, name, rows=None):
    M = a.shape[0] if rows is None else rows
    K, N = b.shape
    osz = jnp.dtype(out_dtype).itemsize
    tm, tn = _mm_tiles(M, K, N, osz)
    vm = _mm_vmem_mib(tm, tn, K, osz)
    return pl.pallas_call(
        _mm_kernel,
        out_shape=jax.ShapeDtypeStruct((M, N), out_dtype),
        grid=(M // tm, N // tn),
        in_specs=[pl.BlockSpec((tm, K), lambda i, j: (i, 0)),
                  pl.BlockSpec((K, tn), lambda i, j: (0, j))],
        out_specs=pl.BlockSpec((tm, tn), lambda i, j: (i, j)),
        compiler_params=_params(("arbitrary", "arbitrary"), vm),
        name=name,
    )(a, b)


def _mm_res_kernel(a_ref, b_ref, x_ref, g_ref, o_ref):
    acc = jnp.dot(a_ref[...], b_ref[...], preferred_element_type=F32)
    o_ref[...] = x_ref[...] + g_ref[...] * acc


def _mm_res_call(cfg, a, b, x, mods, k_gate, rows, name):
    K, N = b.shape
    tm = _pick(math.gcd(cfg.S, cfg.Tc), (512, 256, 128))
    tn = _pick(N, (1024, 512, 256, 128))
    seg = _seg_of_tile(cfg, tm)
    vm = 2 * (tm * K * 2 + K * tn * 2 + 2 * tm * tn * 4) / MIB + tm * tn * 4 / MIB + 4
    return pl.pallas_call(
        _mm_res_kernel,
        out_shape=jax.ShapeDtypeStruct((rows, N), F32),
        grid=(rows // tm, N // tn),
        in_specs=[pl.BlockSpec((tm, K), lambda i, j: (i, 0)),
                  pl.BlockSpec((K, tn), lambda i, j: (0, j)),
                  pl.BlockSpec((tm, tn), lambda i, j: (i, j)),
                  pl.BlockSpec((None, None, 1, tn), lambda i, j: (seg(i), k_gate, 0, j))],
        out_specs=pl.BlockSpec((tm, tn), lambda i, j: (i, j)),
        compiler_params=_params(("arbitrary", "arbitrary"), vm),
        name=name,
    )(a, b, x, mods)


def _key_block(cfg, tm):
    n_lat = cfg.Tx // tm
    per_b = cfg.S // tm
    per_c = cfg.C // tm

    def batch(i):
        return jnp.where(i < n_lat, i // per_b, (i - n_lat) // per_c)

    def row(i):
        return jnp.where(i < n_lat, per_c + i % per_b, (i - n_lat) % per_c)

    return batch, row


def _swap_halves(x, lane, half):
    first = (lane % (2 * half)) < half
    return jnp.where(first, pltpu.roll(x, LANES - half, 1), pltpu.roll(x, half, 1))


def _da_prep_kernel(q_ref, k_ref, v_ref, cos_ref, sin_ref, qo_ref, ko_ref, vo_ref, *, groups, qscale):
    cos = cos_ref[...]
    sin = sin_ref[...]
    lane = lax.broadcasted_iota(jnp.int32, cos.shape, 1)

    def rope(x):
        return x * cos + _swap_halves(x, lane, 32) * sin

    for g in range(groups):
        sl = slice(g * LANES, (g + 1) * LANES)
        qo_ref[:, sl] = (rope(q_ref[:, sl].astype(F32)) * qscale).astype(BF16)
        ko_ref[:, sl] = rope(k_ref[:, sl].astype(F32)).astype(BF16)
    vo_ref[...] = v_ref[...]


def _da_prep_call(cfg, z_da, cos, sin):
    W = cfg.da_w
    tm = cfg.C if cfg.C <= 256 else 256
    kb, kr = _key_block(cfg, tm)
    tok = lambda c: pl.BlockSpec((tm, W), lambda i: (i, c))
    tab = pl.BlockSpec((tm, LANES), lambda i: (i, 0))
    key = pl.BlockSpec((None, tm, W), lambda i: (kb(i), kr(i), 0))
    return pl.pallas_call(
        functools.partial(_da_prep_kernel, groups=W // LANES, qscale=cfg.da_dk ** -0.5 * LOG2E),
        out_shape=(jax.ShapeDtypeStruct((cfg.T, W), BF16),
                   jax.ShapeDtypeStruct((cfg.B, cfg.Sk, W), BF16),
                   jax.ShapeDtypeStruct((cfg.B, cfg.Sk, W), BF16)),
        grid=(cfg.T // tm,),
        in_specs=[tok(0), tok(1), tok(2), tab, tab],
        out_specs=(tok(0), key, key),
        compiler_params=_params(("arbitrary",), 12 * tm * W * 2 / MIB + 8),
        name="da_prep",
    )(z_da, z_da, z_da, cos, sin)


def _mla_prep_kernel(z_ref, gq_ref, gkv_ref, wq_ref, wkv_ref, cos_ref, sin_ref,
                     qo_ref, ko_ref, vo_ref, *, heads, q_rank, kv_rank, qscale):
    z = z_ref[...].astype(F32)
    cos = cos_ref[...]
    sin = sin_ref[...]
    lane = lax.broadcasted_iota(jnp.int32, cos.shape, 1)

    def rope(x):
        return x * cos + _swap_halves(x, lane, 16) * sin

    cq = _rms(z[:, :q_rank], gq_ref[...]).astype(BF16)
    ckv = _rms(z[:, q_rank:q_rank + kv_rank], gkv_ref[...]).astype(BF16)
    kpe = rope(z[:, q_rank + kv_rank:q_rank + kv_rank + LANES]).astype(BF16)
    q = jnp.dot(cq, wq_ref[...], preferred_element_type=F32)
    kv = jnp.dot(ckv, wkv_ref[...], preferred_element_type=F32)
    for h in range(heads):
        a, b, c = 2 * h * LANES, (2 * h + 1) * LANES, (2 * h + 2) * LANES
        qo_ref[:, a:b] = (q[:, a:b] * qscale).astype(BF16)
        qo_ref[:, b:c] = (rope(q[:, b:c]) * qscale).astype(BF16)
        ko_ref[:, a:b] = kv[:, a:b].astype(BF16)
        ko_ref[:, b:c] = kpe
        vo_ref[:, a:b] = kv[:, b:c].astype(BF16)
        vo_ref[:, b:c] = jnp.ones((z.shape[0], LANES), BF16)


def _mla_prep_call(cfg, z_mla, gq, gkv, wq, wkv, cos, sin):
    H = cfg.mla_heads
    Wz = z_mla.shape[1]
    tm = cfg.C if cfg.C <= 256 else 256
    kb, kr = _key_block(cfg, tm)
    full = lambda a: pl.BlockSpec(a.shape, lambda i: (0,) * a.ndim)
    tab = pl.BlockSpec((tm, LANES), lambda i: (i, 0))
    return pl.pallas_call(
        functools.partial(_mla_prep_kernel, heads=H, q_rank=cfg.q_rank, kv_rank=cfg.kv_rank,
                          qscale=(cfg.nope + cfg.rope) ** -0.5 * LOG2E),
        out_shape=(jax.ShapeDtypeStruct((cfg.T, H * 256), BF16),
                   jax.ShapeDtypeStruct((cfg.B, cfg.Sk, H * 256), BF16),
                   jax.ShapeDtypeStruct((cfg.B, cfg.Sk, H * 256), BF16)),
        grid=(cfg.T // tm,),
        in_specs=[pl.BlockSpec((tm, Wz), lambda i: (i, 0)), full(gq), full(gkv), full(wq), full(wkv), tab, tab],
        out_specs=(pl.BlockSpec((tm, H * 256), lambda i: (i, 0)),
                   pl.BlockSpec((None, tm, H * 256), lambda i: (kb(i), kr(i), 0)),
                   pl.BlockSpec((None, tm, H * 256), lambda i: (kb(i), kr(i), 0))),
        compiler_params=_params(("arbitrary",), 2 * (wq.size + wkv.size) * 2 / MIB + 24 * tm * H * 256 / MIB + 8),
        name="mla_prep",
    )(z_mla, gq, gkv, wq, wkv, cos, sin)


def _online_step(q, k, v, m, l, acc):
    s = lax.dot_general(q, k, NT_DIMS, preferred_element_type=F32)
    m_new = jnp.maximum(m, jnp.max(s, axis=-1, keepdims=True))
    alpha = jnp.exp2(m - m_new)
    p = jnp.exp2(s - m_new)
    l = alpha * l + jnp.sum(p, axis=-1, keepdims=True)
    acc = alpha * acc + jnp.dot(p.astype(BF16), v, preferred_element_type=F32)
    return m_new, l, acc


def _da_attn_kernel(lam_ref, g_ref, q_ref, k_ref, v_ref, o_ref, *, tk, nk, lam_init):
    tq = q_ref.shape[0]
    q1 = q_ref[:, :LANES]
    q2 = q_ref[:, LANES:]

    neg = jnp.full((tq, 1), -jnp.inf, F32)
    zero = jnp.zeros((tq, 1), F32)
    acc0 = jnp.zeros((tq, v_ref.shape[1]), F32)
    m1, l1, a1, m2, l2, a2 = neg, zero, acc0, neg, zero, acc0
    for kk in range(nk):
        rows = slice(kk * tk, (kk + 1) * tk)
        v = v_ref[rows, :]
        m1, l1, a1 = _online_step(q1, k_ref[rows, :LANES], v, m1, l1, a1)
        m2, l2, a2 = _online_step(q2, k_ref[rows, LANES:], v, m2, l2, a2)
    lv = lam_ref[...]
    lam = (jnp.exp(jnp.sum(lv[0:1] * lv[1:2], axis=-1, keepdims=True))
           - jnp.exp(jnp.sum(lv[2:3] * lv[3:4], axis=-1, keepdims=True)) + lam_init)
    o = a1 / l1 - lam * (a2 / l2)
    o_ref[...] = (_rms(o, g_ref[...]) * (1.0 - lam_init)).astype(o_ref.dtype)


def _into_mix(kernel, n_in, mix, mix_shape):
    out_shape = jax.ShapeDtypeStruct(mix_shape, BF16)
    if mix is None:
        return kernel, [], (), {}, out_shape

    def with_unused_mix_ref(*refs):
        return kernel(*refs[:n_in], *refs[n_in + 1:])

    return with_unused_mix_ref, [pl.BlockSpec(memory_space=pl.ANY)], (mix,), {n_in: 0}, out_shape


def _da_attn_call(cfg, q, k, v, lam_vec, subln_g, lam_init, mix, mix_shape, *, q_row0, sq, sk):
    H, dv = cfg.da_heads, cfg.da_dv
    tq = _pick(sq, (512, 256, 128))
    tk = _pick(sk, (768, 512, 256, 128))
    nq = sq // tq
    q0 = q_row0 // tq
    vm = 2 * (2 * sk * 256 * 2) / MIB + 16 * tq * tk * 4 / MIB + 12
    kern, mix_spec, mix_arg, alias, out_shape = _into_mix(
        functools.partial(_da_attn_kernel, tk=tk, nk=sk // tk, lam_init=lam_init), 5, mix, mix_shape)
    return pl.pallas_call(
        kern,
        out_shape=out_shape,
        input_output_aliases=alias,
        grid=(cfg.B, H, nq),
        in_specs=[pl.BlockSpec((4, LANES), lambda b, h, i: (0, 0)),
                  pl.BlockSpec((1, dv), lambda b, h, i: (0, 0)),
                  pl.BlockSpec((tq, 256), lambda b, h, i: (q0 + b * nq + i, h)),
                  pl.BlockSpec((None, sk, 256), lambda b, h, i: (b, 0, h)),
                  pl.BlockSpec((None, sk, dv), lambda b, h, i: (b, 0, h))] + mix_spec,
        out_specs=pl.BlockSpec((tq, dv), lambda b, h, i: (q0 + b * nq + i, h)),
        compiler_params=_params(("arbitrary", "arbitrary", "arbitrary"), vm),
        name="diff_attention",
    )(lam_vec, subln_g.reshape(1, dv), q, k, v, *mix_arg)


def _mla_attn_kernel(q_ref, k_ref, v_ref, o_ref, *, tk, nk):
    tq = q_ref.shape[0]
    q = q_ref[...]

    m, acc = jnp.full((tq, 1), -jnp.inf, F32), jnp.zeros((tq, v_ref.shape[1]), F32)
    for kk in range(nk):
        rows = slice(kk * tk, (kk + 1) * tk)
        s = lax.dot_general(q, k_ref[rows, :], NT_DIMS, preferred_element_type=F32)
        m_new = jnp.maximum(m, jnp.max(s, axis=-1, keepdims=True))
        p = jnp.exp2(s - m_new).astype(BF16)
        acc = jnp.exp2(m - m_new) * acc + jnp.dot(p, v_ref[rows, :], preferred_element_type=F32)
        m = m_new
    dv = o_ref.shape[1]
    o_ref[...] = (acc[:, :dv] / acc[:, dv:dv + 1]).astype(o_ref.dtype)


def _mla_attn_call(cfg, q, k, v, mix, mix_shape, *, q_row0, sq, sk):
    H, dv = cfg.mla_heads, cfg.mla_dv
    col0 = cfg.da_w // dv
    tq = _pick(sq, (512, 256, 128))
    tk = _pick(sk, (768, 512, 256, 128))
    nq = sq // tq
    q0 = q_row0 // tq
    vm = 2 * (sk * 512 * 2) / MIB + 6 * tq * tk * 4 / MIB + 16
    kern, mix_spec, mix_arg, alias, out_shape = _into_mix(
        functools.partial(_mla_attn_kernel, tk=tk, nk=sk // tk), 3, mix, mix_shape)
    return pl.pallas_call(
        kern,
        out_shape=out_shape,
        input_output_aliases=alias,
        grid=(cfg.B, H, nq),
        in_specs=[pl.BlockSpec((tq, 256), lambda b, h, i: (q0 + b * nq + i, h)),
                  pl.BlockSpec((None, sk, 256), lambda b, h, i: (b, 0, h)),
                  pl.BlockSpec((None, sk, 2 * dv), lambda b, h, i: (b, 0, h))] + mix_spec,
        out_specs=pl.BlockSpec((tq, dv), lambda b, h, i: (q0 + b * nq + i, col0 + h)),
        compiler_params=_params(("arbitrary", "arbitrary", "arbitrary"), vm),
        name="mla_attention",
    )(q, k, v, *mix_arg)


def _chunk_mlp_kernel(z_ref, g_ref, ws_ref, b_ref, o_ref, *, groups, chunk):
    uv = _gelu(z_ref[...].astype(F32))
    w = groups * LANES
    u = uv[:, :w]
    v = _rms(uv[:, w:], g_ref[...]).astype(BF16)
    for c in range(z_ref.shape[0] // chunk):
        rows = slice(c * chunk, (c + 1) * chunk)
        for g in range(groups):
            cols = slice(g * LANES, (g + 1) * LANES)
            y = jnp.dot(ws_ref[g], v[rows, cols], preferred_element_type=F32) + b_ref[:, cols]
            o_ref[rows, cols] = (u[rows, cols] * y).astype(o_ref.dtype)


def _chunk_mlp_call(cfg, z_cm, g, ws, b_full, mix, mix_shape, rows):
    W = cfg.cm_w
    col0 = (cfg.da_w + cfg.mla_w) // W
    tm = _pick(math.gcd(cfg.S, cfg.Tc), (256, 128))
    kern, mix_spec, mix_arg, alias, out_shape = _into_mix(
        functools.partial(_chunk_mlp_kernel, groups=cfg.cm_groups, chunk=cfg.cm_chunk), 4, mix, mix_shape)
    return pl.pallas_call(
        kern,
        out_shape=out_shape,
        input_output_aliases=alias,
        grid=(rows // tm,),
        in_specs=[pl.BlockSpec((tm, 2 * W), lambda i: (i, 0)),
                  pl.BlockSpec((1, W), lambda i: (0, 0)),
                  pl.BlockSpec(ws.shape, lambda i: (0, 0, 0)),
                  pl.BlockSpec(b_full.shape, lambda i: (0, 0))] + mix_spec,
        out_specs=pl.BlockSpec((tm, W), lambda i: (i, col0)),
        compiler_params=_params(("arbitrary",), 40 * tm * W / MIB + 8),
        name="chunk_mlp",
    )(z_cm, g.reshape(1, W), ws, b_full, *mix_arg)


INT32_MIN = -2 ** 31


def _order_key(x):
    b = lax.bitcast_convert_type(x, jnp.int32)
    return jnp.where(b < 0, b ^ 0x7FFFFFFF, b)


def _order_key_inv(k):
    return lax.bitcast_convert_type(jnp.where(k < 0, k ^ 0x7FFFFFFF, k), F32)


def _top_k_sublanes(keys, pos, topk):
    iota_k = lax.broadcasted_iota(jnp.int32, (topk, keys[0].shape[1]), 0)

    def body(k, carry):
        out = []
        for (work, vals), p in zip(carry, pos):
            m = jnp.max(work, axis=0, keepdims=True)
            idx = jnp.min(jnp.where(work == m, p, jnp.inf), axis=0, keepdims=True)
            out.append((jnp.where(p == idx, INT32_MIN + k, work), jnp.where(iota_k == k, m, vals)))
        return tuple(out)

    init = tuple((w, jnp.zeros((topk, w.shape[1]), jnp.int32)) for w in keys)
    res = lax.fori_loop(0, topk, body, init)
    return [(jnp.where(w < INT32_MIN + topk, w - INT32_MIN, topk), v) for w, v in res]


def _pair_blocks(topk):
    return [(0, topk)] + [(a, 8) for a in range(1, 8)]


def _route_kernel(q_ref, keys_ref, er_ref, cnt_ref, ec_ref, rk_ref, *, topk):
    half = q_ref.shape[1] // 2
    tm = q_ref.shape[0]
    s_row = lax.dot_general(keys_ref[0], q_ref[:, :half], NT_DIMS, preferred_element_type=F32)
    s_col = lax.dot_general(keys_ref[1], q_ref[:, half:], NT_DIMS, preferred_element_type=F32)
    iota = lax.broadcasted_iota(jnp.int32, (s_row.shape[0], LANES), 0).astype(F32)
    ranks_r, ranks_c, tops_r, tops_c = [], [], [], []
    for g in range(tm // LANES):
        lanes = slice(g * LANES, (g + 1) * LANES)
        (rr, vr), (rc, vc) = _top_k_sublanes([_order_key(s_row[:, lanes]), _order_key(s_col[:, lanes])],
                                             [iota, iota], topk)
        ranks_r.append(rr)
        ranks_c.append(rc)
        tops_r.append(_order_key_inv(vr))
        tops_c.append(_order_key_inv(vc))
    rank_r = jnp.concatenate(ranks_r, axis=1)
    rank_c = jnp.concatenate(ranks_c, axis=1)
    top_r = jnp.concatenate(tops_r, axis=1)
    top_c = jnp.concatenate(tops_c, axis=1)

    blocks = _pair_blocks(topk)
    cand_parts, pos_parts, valid_parts = [], [], []
    for a, nb in blocks:
        b_iota = lax.broadcasted_iota(jnp.int32, (nb, tm), 0)
        cand_parts.append(top_r[a:a + 1, :] + top_c[:nb, :])
        pos_parts.append((a * topk + b_iota).astype(F32))
        valid_parts.append((a + 1) * (b_iota + 1) <= topk)
    a_iota = lax.broadcasted_iota(jnp.int32, (topk - 8, tm), 0) + 8
    cand_parts.append(top_r[8:, :] + top_c[0:1, :])
    pos_parts.append((a_iota * topk).astype(F32))
    valid_parts.append(a_iota < topk)
    cand = jnp.concatenate(cand_parts, axis=0)
    pos = jnp.concatenate(pos_parts, axis=0)
    valid = jnp.concatenate(valid_parts, axis=0)
    cand_key = jnp.where(valid, _order_key(cand), INT32_MIN + topk)
    (rank_p, best), = _top_k_sublanes([cand_key], [pos], topk)
    chosen = rank_p < topk
    best0 = _order_key_inv(best[0:1, :])
    z = jnp.sum(jnp.where(chosen, jnp.exp(cand - best0), 0.0), axis=0, keepdims=True)
    chosen_f = chosen.astype(F32)
    cnt = jnp.zeros(s_row.shape, F32)
    row0 = 0
    for a, nb in blocks:
        cnt_a = jnp.sum(chosen_f[row0:row0 + nb, :], axis=0, keepdims=True)
        cnt = jnp.where(rank_r == a, cnt_a, cnt)
        row0 += nb
    for a in range(8, topk):
        cnt = jnp.where(rank_r == a, chosen_f[row0 + a - 8:row0 + a - 7, :], cnt)
    er_ref[...] = jnp.where(rank_r < topk, jnp.exp(s_row - top_r[0:1, :]), 0.0)
    cnt_ref[...] = cnt
    ec_ref[...] = jnp.where(rank_c < topk, jnp.exp(s_col - top_c[0:1, :]) / z, 0.0)
    rk_ref[...] = rank_c.astype(F32)


def _route_call(cfg, qp, keys, rows):
    H, nk = cfg.peer_heads, cfg.peer_keys
    tm = _pick(rows, (256, 128))
    out = lambda dt: jax.ShapeDtypeStruct((H, nk, rows), dt)
    ospec = pl.BlockSpec((None, nk, tm), lambda i, h: (h, 0, i))
    return pl.pallas_call(
        functools.partial(_route_kernel, topk=cfg.topk),
        out_shape=(out(F32), out(F32), out(F32), out(F32)),
        grid=(rows // tm, H),
        in_specs=[pl.BlockSpec((tm, cfg.peer_dkey), lambda i, h: (i, h)),
                  pl.BlockSpec((None, 2, nk, cfg.peer_dkey // 2), lambda i, h: (h, 0, 0, 0))],
        out_specs=(ospec, ospec, ospec, ospec),
        compiler_params=_params(("arbitrary", "arbitrary"), 32),
        name="peer_route",
    )(qp, keys)


def _peer_act_kernel(u_ref, h_ref, er_ref, cnt_ref, ec_ref, rk_ref, o_ref, *, heads):
    act = _gelu(lax.dot_general(u_ref[...], h_ref[...], NT_DIMS, preferred_element_type=F32))
    for r in range(u_ref.shape[0] // LANES):
        rows = slice(r * LANES, (r + 1) * LANES)
        for g in range(h_ref.shape[0] // LANES):
            lanes = slice(g * LANES, (g + 1) * LANES)
            w = jnp.zeros((LANES, LANES), F32)
            for h in range(heads):
                hit = rk_ref[h, :, lanes] < cnt_ref[h, r:r + 1, lanes]
                w = w + jnp.where(hit, ec_ref[h, :, lanes], 0.0) * er_ref[h, r:r + 1, lanes]
            o_ref[rows, lanes] = (w * act[rows, lanes]).astype(o_ref.dtype)


def _peer_act_call(cfg, u, h2, er, cnt, ec, rk, rows):
    H, nk, D = cfg.peer_heads, cfg.peer_keys, cfg.D
    tn = 1024
    tm = _pick(rows, (512, 256, 128))
    R = tn // LANES
    row_spec = pl.BlockSpec((H, R, tm), lambda i, n: (0, n, i))
    col_spec = pl.BlockSpec((H, nk, tm), lambda i, n: (0, 0, i))
    vm = 2 * (tn * D * 2 + tm * D * 2 + tn * tm * 2 + 2 * H * nk * tm * 4) / MIB + 4 * tn * tm * 4 / MIB + 6
    return pl.pallas_call(
        functools.partial(_peer_act_kernel, heads=H),
        out_shape=jax.ShapeDtypeStruct((cfg.n_exp, rows), BF16),
        grid=(rows // tm, cfg.n_exp // tn),
        in_specs=[pl.BlockSpec((tn, D), lambda i, n: (n, 0)),
                  pl.BlockSpec((tm, D), lambda i, n: (i, 0)),
                  row_spec, row_spec, col_spec, col_spec],
        out_specs=pl.BlockSpec((tn, tm), lambda i, n: (n, i)),
        compiler_params=_params(("arbitrary", "arbitrary"), vm),
        name="peer_activations",
    )(u, h2, er, cnt, ec, rk)


def _peer_out_kernel(pt_ref, v_ref, x_ref, g_ref, o_ref, acc_ref):
    k = pl.program_id(2)

    @pl.when(k == 0)
    def _():
        acc_ref[...] = jnp.zeros_like(acc_ref)

    acc_ref[...] += lax.dot_general(pt_ref[...], v_ref[...], TN_DIMS, preferred_element_type=F32)

    @pl.when(k == pl.num_programs(2) - 1)
    def _():
        o_ref[...] = x_ref[...] + g_ref[...] * acc_ref[...]


def _peer_out_call(cfg, pt, v, x, mods, k_gate, rows):
    D = cfg.D
    tm = _pick(math.gcd(cfg.S, cfg.Tc), (512, 256, 128))
    tn = _pick(D, (1024, 512, 256, 128))
    tk = _pick(cfg.n_exp, (4096, 2048, 1024))
    seg = _seg_of_tile(cfg, tm)
    vm = 2 * (tk * tm * 2 + tk * tn * 2 + 2 * tm * tn * 4) / MIB + 3 * tm * tn * 4 / MIB + 6
    return pl.pallas_call(
        _peer_out_kernel,
        out_shape=jax.ShapeDtypeStruct((rows, D), F32),
        grid=(rows // tm, D // tn, cfg.n_exp // tk),
        in_specs=[pl.BlockSpec((tk, tm), lambda i, j, k: (k, i)),
                  pl.BlockSpec((tk, tn), lambda i, j, k: (k, j)),
                  pl.BlockSpec((tm, tn), lambda i, j, k: (i, j)),
                  pl.BlockSpec((None, None, 1, tn), lambda i, j, k: (seg(i), k_gate, 0, j))],
        out_specs=pl.BlockSpec((tm, tn), lambda i, j, k: (i, j)),
        scratch_shapes=[pltpu.VMEM((tm, tn), F32)],
        compiler_params=_params(("arbitrary", "arbitrary", "arbitrary"), vm),
        name="peer_output",
    )(pt, v, x, mods)


def _final_norm_kernel(x_ref, g_ref, o_ref):
    o_ref[...] = _rms(x_ref[...], g_ref[...])


def _final_norm_call(cfg, x, g):
    D = cfg.D
    tm = _pick(cfg.Tx, (512, 256, 128))
    return pl.pallas_call(
        _final_norm_kernel,
        out_shape=jax.ShapeDtypeStruct((cfg.Tx, D), F32),
        grid=(cfg.Tx // tm,),
        in_specs=[pl.BlockSpec((tm, D), lambda i: (i, 0)), pl.BlockSpec((1, D), lambda i: (0, 0))],
        out_specs=pl.BlockSpec((tm, D), lambda i: (i, 0)),
        compiler_params=_params(("arbitrary",), 6 * tm * D * 4 / MIB + 4),
        name="final_norm",
    )(x, g.reshape(1, D))


def _rope_tables(cfg, half, pad):
    pos = jnp.arange(cfg.S, dtype=jnp.int32)
    inv = ROPE_BASE ** (-jnp.arange(half, dtype=F32) / half)
    ang_r = (pos // cfg.grid_w).astype(F32)[:, None] * inv[None, :]
    ang_c = (pos % cfg.grid_w).astype(F32)[:, None] * inv[None, :]
    cos = jnp.concatenate([jnp.cos(ang_r)] * 2 + [jnp.cos(ang_c)] * 2, axis=-1)
    sin = jnp.concatenate([-jnp.sin(ang_r), jnp.sin(ang_r), -jnp.sin(ang_c), jnp.sin(ang_c)], axis=-1)
    cos = jnp.concatenate([jnp.tile(cos, (cfg.B, 1)), jnp.ones((cfg.Tc, 4 * half), F32)], axis=0)
    sin = jnp.concatenate([jnp.tile(sin, (cfg.B, 1)), jnp.zeros((cfg.Tc, 4 * half), F32)], axis=0)
    if pad:
        cos = jnp.pad(cos, ((0, 0), (0, pad)))
        sin = jnp.pad(sin, ((0, 0), (0, pad)))
    return cos, sin


def _forward(cfg, x, c, ctx, c_ctx, w_mod, b_mod, norm1_g, norm2_g, w_in, da_lambda, da_subln_g,
             mla_q_norm_g, mla_kv_norm_g, mla_w_uq, mla_w_ukv, cm_v_norm_g, cm_w_s, cm_b_s,
             w_out, peer_w_q, peer_sub_keys, peer_u, peer_v, final_norm_g):
    B, S, C, D, L = cfg.B, cfg.S, cfg.C, cfg.D, cfg.L
    assert cfg.da_dk == LANES and cfg.nope == LANES and cfg.peer_keys == LANES and cfg.topk == 16
    assert C % LANES == 0 and S % C == 0 and cfg.q_rank % LANES == 0 and cfg.kv_rank % LANES == 0
    assert (cfg.da_w + cfg.mla_w) % cfg.cm_w == 0

    xs = jnp.concatenate([x.reshape(B * S, D), ctx.reshape(B * C, D)], axis=0)
    c_all = jnp.concatenate([c, c_ctx[None, :], jnp.zeros((8 - B - 1, D), F32)], axis=0)
    mods_all = _mod_call(cfg, c_all, w_mod, b_mod).reshape(L, 8, 6, 1, D)
    cos_da, sin_da = _rope_tables(cfg, 32, 0)
    cos_ml, sin_ml = _rope_tables(cfg, 16, LANES - cfg.rope)

    for l in range(L):
        last = l == L - 1
        rows = cfg.Tx if last else cfg.T
        mods = mods_all[l]
        lam_init = 0.8 - 0.6 * math.exp(-0.3 * l)

        w_in_l = w_in[l]
        w_da = w_in_l[:, :cfg.off_cq].astype(BF16)
        ml_cols = cfg.off_cm - cfg.off_cq
        w_ml = jnp.pad(w_in_l[:, cfg.off_cq:cfg.off_cm], ((0, 0), (0, -ml_cols % 256))).astype(BF16)
        w_cm = w_in_l[:, cfg.off_cm:].astype(BF16)
        H = cfg.mla_heads
        wq = jnp.pad(mla_w_uq[l].reshape(cfg.q_rank, H, cfg.nope + cfg.rope),
                     ((0, 0), (0, 0), (0, LANES - cfg.rope))).reshape(cfg.q_rank, H * 256).astype(BF16)
        wkv = mla_w_ukv[l].astype(BF16)
        b_full = jnp.repeat(cm_b_s[l].T, cfg.cm_dg, axis=1)
        keys = peer_sub_keys[l].astype(BF16)

        h1 = _norm_mod_call(cfg, xs, norm1_g[l], mods, 0, 1, cfg.T)
        z_da = _mm_call(h1, w_da, BF16, "in_proj_da")
        z_ml = _mm_call(h1, w_ml, BF16, "in_proj_mla")
        z_cm = _mm_call(h1, w_cm, BF16, "in_proj_cm", rows=rows)
        q_da, k_da, v_da = _da_prep_call(cfg, z_da, cos_da, sin_da)
        q_ml, k_ml, v_ml = _mla_prep_call(cfg, z_ml, mla_q_norm_g[l].reshape(1, -1), mla_kv_norm_g[l].reshape(1, -1),
                                          wq, wkv, cos_ml, sin_ml)
        mshape = (rows, cfg.mix_w)
        da_args = (cfg, q_da, k_da, v_da, da_lambda[l], da_subln_g[l], lam_init)
        mix = _da_attn_call(*da_args, None, mshape, q_row0=0, sq=S, sk=cfg.Sk)
        mix = _mla_attn_call(cfg, q_ml, k_ml, v_ml, mix, mshape, q_row0=0, sq=S, sk=cfg.Sk)
        if not last:
            mix = _da_attn_call(*da_args, mix, mshape, q_row0=cfg.Tx, sq=C, sk=C)
            mix = _mla_attn_call(cfg, q_ml, k_ml, v_ml, mix, mshape, q_row0=cfg.Tx, sq=C, sk=C)
        mix = _chunk_mlp_call(cfg, z_cm, cm_v_norm_g[l], cm_w_s[l].astype(BF16), b_full, mix, mshape, rows)
        xs = _mm_res_call(cfg, mix, _cast_call(w_out, l), xs, mods, 2, rows, "out_proj")

        h2 = _norm_mod_call(cfg, xs, norm2_g[l], mods, 3, 4, rows)
        qp = _mm_call(h2, _cast_call(peer_w_q, l), BF16, "peer_query")
        er, cnt, ec, rk = _route_call(cfg, qp, keys, rows)
        pt = _peer_act_call(cfg, _cast_call(peer_u, l), h2, er, cnt, ec, rk, rows)
        xs = _peer_out_call(cfg, pt, _cast_call(peer_v, l), xs, mods, 5, rows)

    return _final_norm_call(cfg, xs, final_norm_g).reshape(B, S, D)


def kernel(x, c, ctx, c_ctx, w_mod, b_mod, norm1_g, norm2_g, w_in, da_lambda, da_subln_g, mla_q_norm_g,
           mla_kv_norm_g, mla_w_uq, mla_w_ukv, cm_v_norm_g, cm_w_s, cm_b_s, w_out, peer_w_q, peer_sub_keys,
           peer_u, peer_v, final_norm_g):
    return _forward(_Cfg(), x, c, ctx, c_ctx, w_mod, b_mod, norm1_g, norm2_g, w_in, da_lambda, da_subln_g,
                    mla_q_norm_g, mla_kv_norm_g, mla_w_uq, mla_w_ukv, cm_v_norm_g, cm_w_s, cm_b_s, w_out,
                    peer_w_q, peer_sub_keys, peer_u, peer_v, final_norm_g)
```

```python
import functools
import math

import jax
import jax.numpy as jnp
from jax import lax
from jax.experimental import pallas as pl
from jax.experimental.pallas import tpu as pltpu

F32 = jnp.float32
BF16 = jnp.bfloat16
EPS = 1e-6
ROPE_BASE = 10000.0
LANES = 128
MIB = 1024 * 1024
LOG2E = math.log2(math.e)
NT_DIMS = (((1,), (1,)), ((), ()))
TN_DIMS = (((0,), (0,)), ((), ()))


class _Cfg:
    def __init__(self, d_model=4096, batch=2, seq=8192, depth=2, grid_w=64, ctx_len=256,
                 da_heads=6, mla_heads=12, mla_q_rank=768, mla_kv_rank=512,
                 cm_groups=8, peer_heads=8):
        self.D, self.B, self.S, self.L = d_model, batch, seq, depth
        self.grid_w, self.C = grid_w, ctx_len
        self.da_heads, self.da_dk, self.da_dv = da_heads, 128, 256
        self.mla_heads, self.q_rank, self.kv_rank = mla_heads, mla_q_rank, mla_kv_rank
        self.nope, self.rope, self.mla_dv = 128, 64, 128
        self.cm_groups, self.cm_chunk, self.cm_dg = cm_groups, 128, 128
        self.peer_heads, self.peer_keys, self.peer_dkey, self.topk = peer_heads, 128, 256, 16
        self.da_w = da_heads * self.da_dv
        self.mla_w = mla_heads * self.mla_dv
        self.cm_w = cm_groups * self.cm_dg
        self.mix_w = self.da_w + self.mla_w + self.cm_w
        self.off_da_k = da_heads * 2 * self.da_dk
        self.off_da_v = 2 * self.off_da_k
        self.off_cq = self.off_da_v + self.da_w
        self.off_ckv = self.off_cq + mla_q_rank
        self.off_kr = self.off_ckv + mla_kv_rank
        self.off_cm = self.off_kr + self.rope
        self.in_cols = self.off_cm + 2 * self.cm_w
        self.n_exp = self.peer_keys * self.peer_keys
        self.Tx = batch * seq
        self.Tc = batch * ctx_len
        self.T = self.Tx + self.Tc
        self.Sk = ctx_len + seq


def _pick(n, cands):
    for c in cands:
        if n % c == 0:
            return c
    raise ValueError(f"no tile in {cands} divides {n}")


def _params(sem, vmem_mib, flags=None):
    return pltpu.CompilerParams(dimension_semantics=sem, vmem_limit_bytes=int(vmem_mib * MIB), flags=flags)


def _rms(x, g):
    return x * lax.rsqrt(jnp.mean(x * x, axis=-1, keepdims=True) + EPS) * g


def _gelu(x):
    return 0.5 * x * (1.0 + jnp.tanh(math.sqrt(2.0 / math.pi) * (x + 0.044715 * (x * x * x))))


def _mod_kernel(c_ref, w_ref, b_ref, o_ref):
    c = c_ref[...]
    s = (c * jax.nn.sigmoid(c)).astype(BF16)
    o_ref[...] = jnp.dot(s, w_ref[...].astype(BF16), preferred_element_type=F32) + b_ref[...]


def _mod_call(cfg, c_all, w_mod, b_mod):
    L, D = cfg.L, cfg.D
    n = 6 * D
    tn = _pick(n, (512, 256, 128))
    return pl.pallas_call(
        _mod_kernel,
        out_shape=jax.ShapeDtypeStruct((L, 8, n), F32),
        grid=(L, n // tn),
        in_specs=[pl.BlockSpec((8, D), lambda l, j: (0, 0)),
                  pl.BlockSpec((None, D, tn), lambda l, j: (l, 0, j)),
                  pl.BlockSpec((None, 1, tn), lambda l, j: (l, 0, j))],
        out_specs=pl.BlockSpec((None, 8, tn), lambda l, j: (l, 0, j)),
        compiler_params=_params(("arbitrary", "arbitrary"), 2 * D * tn * 4 / MIB + 3 * D * tn * 2 / MIB + 4),
        name="adaln_mod",
    )(c_all, w_mod, b_mod.reshape(L, 1, n))


def _cast_kernel(w_ref, o_ref):
    o_ref[...] = w_ref[...].astype(o_ref.dtype)


def _cast_call(w, l):
    _, R, Cw = w.shape
    tr = _pick(R, (512, 256, 128))
    return pl.pallas_call(
        _cast_kernel,
        out_shape=jax.ShapeDtypeStruct((R, Cw), BF16),
        grid=(R // tr,),
        in_specs=[pl.BlockSpec((None, tr, Cw), lambda i: (l, i, 0))],
        out_specs=pl.BlockSpec((tr, Cw), lambda i: (i, 0)),
        compiler_params=_params(("arbitrary",), 2 * tr * Cw * 6 / MIB + 4),
        name="weight_cast",
    )(w)


def _norm_mod_kernel(x_ref, g_ref, sh_ref, sc_ref, o_ref):
    y = _rms(x_ref[...], g_ref[...])
    o_ref[...] = (y * (1.0 + sc_ref[...]) + sh_ref[...]).astype(o_ref.dtype)


def _res_norm_mod_kernel(x_ref, r_ref, gt_ref, g_ref, sh_ref, sc_ref, xo_ref, o_ref):
    x = x_ref[...] + gt_ref[...] * r_ref[...]
    xo_ref[...] = x
    o_ref[...] = (_rms(x, g_ref[...]) * (1.0 + sc_ref[...]) + sh_ref[...]).astype(o_ref.dtype)


def _seg_of_tile(cfg, tm):
    return lambda i: jnp.minimum((i * tm) // cfg.S, cfg.B)


def _res_norm_mod_call(cfg, x, raw, gate_mods, k_gate, g, mods, k_shift, k_scale, rows):
    D = cfg.D
    tm = _pick(math.gcd(cfg.S, cfg.Tc), (256, 128))
    seg = _seg_of_tile(cfg, tm)
    row = pl.BlockSpec((tm, D), lambda i: (i, 0))
    vec = lambda k: pl.BlockSpec((None, None, 1, D), lambda i: (seg(i), k, 0, 0))
    return pl.pallas_call(
        _res_norm_mod_kernel,
        out_shape=(jax.ShapeDtypeStruct((rows, D), F32), jax.ShapeDtypeStruct((rows, D), BF16)),
        grid=(rows // tm,),
        in_specs=[row, row, vec(k_gate), pl.BlockSpec((1, D), lambda i: (0, 0)), vec(k_shift), vec(k_scale)],
        out_specs=(row, row),
        compiler_params=_params(("arbitrary",), 12 * tm * D * 4 / MIB + 4),
        name="residual_norm_modulate",
    )(x, raw, gate_mods, g.reshape(1, D), mods, mods)


def _norm_mod_call(cfg, x, g, mods, k_shift, k_scale, rows):
    D = cfg.D
    tm = _pick(math.gcd(cfg.S, cfg.Tc), (512, 256, 128))
    seg = _seg_of_tile(cfg, tm)
    return pl.pallas_call(
        _norm_mod_kernel,
        out_shape=jax.ShapeDtypeStruct((rows, D), BF16),
        grid=(rows // tm,),
        in_specs=[pl.BlockSpec((tm, D), lambda i: (i, 0)),
                  pl.BlockSpec((1, D), lambda i: (0, 0)),
                  pl.BlockSpec((None, None, 1, D), lambda i: (seg(i), k_shift, 0, 0)),
                  pl.BlockSpec((None, None, 1, D), lambda i: (seg(i), k_scale, 0, 0))],
        out_specs=pl.BlockSpec((tm, D), lambda i: (i, 0)),
        compiler_params=_params(("arbitrary",), 5 * tm * D * 4 / MIB + 4),
        name="norm_modulate",
    )(x, g.reshape(1, D), mods, mods)


def _mm_kernel(a_ref, b_ref, o_ref):
    o_ref[...] = jnp.dot(a_ref[...], b_ref[...], preferred_element_type=F32).astype(o_ref.dtype)


MM_VMEM_BUDGET_MIB = 48


def _mm_vmem_mib(tm, tn, K, osz):
    return (2 * (tm * K * 2 + K * tn * 2 + tm * tn * osz) + tm * tn * 4) / MIB + 4


def _mm_tiles(M, K, N, osz):
    fits = [(tm * tn / (tm + tn), tm, tn)
            for tm in (1536, 1024, 512, 256, 128) for tn in (N, 1536, 1024, 768, 512, 256, 128)
            if M % tm == 0 and N % tn == 0 and (tn == N or tn % LANES == 0)
            and _mm_vmem_mib(tm, tn, K, osz) <= MM_VMEM_BUDGET_MIB]
    if not fits:
        raise ValueError(f"no matmul tiling for {(M, K, N)}")
    _, tm, tn = max(fits)
    return tm, tn


def _mm_call(a, b, out_dtype, name, rows=None):
    M = a.shape[0] if rows is None else rows
    K, N = b.shape
    osz = jnp.dtype(out_dtype).itemsize
    tm, tn = _mm_tiles(M, K, N, osz)
    vm = _mm_vmem_mib(tm, tn, K, osz)
    return pl.pallas_call(
        _mm_kernel,
        out_shape=jax.ShapeDtypeStruct((M, N), out_dtype),
        grid=(M // tm, N // tn),
        in_specs=[pl.BlockSpec((tm, K), lambda i, j: (i, 0)),
                  pl.BlockSpec((K, tn), lambda i, j: (0, j))],
        out_specs=pl.BlockSpec((tm, tn), lambda i, j: (i, j)),
        compiler_params=_params(("arbitrary", "arbitrary"), vm),
        name=name,
    )(a, b)


def _mm_res_kernel(a_ref, b_ref, x_ref, g_ref, o_ref):
    acc = jnp.dot(a_ref[...], b_ref[...], preferred_element_type=F32)
    o_ref[...] = x_ref[...] + g_ref[...] * acc


def _mm_res_call(cfg, a, b, x, mods, k_gate, rows, name):
    K, N = b.shape
    tm = _pick(math.gcd(cfg.S, cfg.Tc), (512, 256, 128))
    tn = _pick(N, (1024, 512, 256, 128))
    seg = _seg_of_tile(cfg, tm)
    vm = 2 * (tm * K * 2 + K * tn * 2 + 2 * tm * tn * 4) / MIB + tm * tn * 4 / MIB + 4
    return pl.pallas_call(
        _mm_res_kernel,
        out_shape=jax.ShapeDtypeStruct((rows, N), F32),
        grid=(rows // tm, N // tn),
        in_specs=[pl.BlockSpec((tm, K), lambda i, j: (i, 0)),
                  pl.BlockSpec((K, tn), lambda i, j: (0, j)),
                  pl.BlockSpec((tm, tn), lambda i, j: (i, j)),
                  pl.BlockSpec((None, None, 1, tn), lambda i, j: (seg(i), k_gate, 0, j))],
        out_specs=pl.BlockSpec((tm, tn), lambda i, j: (i, j)),
        compiler_params=_params(("arbitrary", "arbitrary"), vm),
        name=name,
    )(a, b, x, mods)


def _key_block(cfg, tm):
    n_lat = cfg.Tx // tm
    per_b = cfg.S // tm
    per_c = cfg.C // tm

    def batch(i):
        return jnp.where(i < n_lat, i // per_b, (i - n_lat) // per_c)

    def row(i):
        return jnp.where(i < n_lat, per_c + i % per_b, (i - n_lat) % per_c)

    return batch, row


def _swap_halves(x, lane, half):
    first = (lane % (2 * half)) < half
    return jnp.where(first, pltpu.roll(x, LANES - half, 1), pltpu.roll(x, half, 1))


def _da_prep_kernel(q_ref, k_ref, v_ref, cos_ref, sin_ref, qo_ref, ko_ref, vo_ref, *, groups, qscale):
    cos = cos_ref[...]
    sin = sin_ref[...]
    lane = lax.broadcasted_iota(jnp.int32, cos.shape, 1)

    def rope(x):
        return x * cos + _swap_halves(x, lane, 32) * sin

    for g in range(groups):
        sl = slice(g * LANES, (g + 1) * LANES)
        qo_ref[:, sl] = (rope(q_ref[:, sl].astype(F32)) * qscale).astype(BF16)
        ko_ref[:, sl] = rope(k_ref[:, sl].astype(F32)).astype(BF16)
    vo_ref[...] = v_ref[...]


def _da_prep_call(cfg, z_da, cos, sin):
    W = cfg.da_w
    tm = cfg.C if cfg.C <= 256 else 256
    kb, kr = _key_block(cfg, tm)
    tok = lambda c: pl.BlockSpec((tm, W), lambda i: (i, c))
    tab = pl.BlockSpec((tm, LANES), lambda i: (i, 0))
    key = pl.BlockSpec((None, tm, W), lambda i: (kb(i), kr(i), 0))
    return pl.pallas_call(
        functools.partial(_da_prep_kernel, groups=W // LANES, qscale=cfg.da_dk ** -0.5 * LOG2E),
        out_shape=(jax.ShapeDtypeStruct((cfg.T, W), BF16),
                   jax.ShapeDtypeStruct((cfg.B, cfg.Sk, W), BF16),
                   jax.ShapeDtypeStruct((cfg.B, cfg.Sk, W), BF16)),
        grid=(cfg.T // tm,),
        in_specs=[tok(0), tok(1), tok(2), tab, tab],
        out_specs=(tok(0), key, key),
        compiler_params=_params(("arbitrary",), 12 * tm * W * 2 / MIB + 8),
        name="da_prep",
    )(z_da, z_da, z_da, cos, sin)


def _mla_prep_kernel(z_ref, gq_ref, gkv_ref, wq_ref, wkv_ref, cos_ref, sin_ref,
                     qo_ref, ko_ref, vo_ref, *, heads, q_rank, kv_rank, qscale):
    z = z_ref[...].astype(F32)
    cos = cos_ref[...]
    sin = sin_ref[...]
    lane = lax.broadcasted_iota(jnp.int32, cos.shape, 1)

    def rope(x):
        return x * cos + _swap_halves(x, lane, 16) * sin

    cq = _rms(z[:, :q_rank], gq_ref[...]).astype(BF16)
    ckv = _rms(z[:, q_rank:q_rank + kv_rank], gkv_ref[...]).astype(BF16)
    kpe = rope(z[:, q_rank + kv_rank:q_rank + kv_rank + LANES]).astype(BF16)
    q = jnp.dot(cq, wq_ref[...], preferred_element_type=F32)
    kv = jnp.dot(ckv, wkv_ref[...], preferred_element_type=F32)
    for h in range(heads):
        a, b, c = 2 * h * LANES, (2 * h + 1) * LANES, (2 * h + 2) * LANES
        qo_ref[:, a:b] = (q[:, a:b] * qscale).astype(BF16)
        qo_ref[:, b:c] = (rope(q[:, b:c]) * qscale).astype(BF16)
        ko_ref[:, a:b] = kv[:, a:b].astype(BF16)
        ko_ref[:, b:c] = kpe
        vo_ref[:, a:b] = kv[:, b:c].astype(BF16)
        vo_ref[:, b:c] = jnp.ones((z.shape[0], LANES), BF16)


def _mla_prep_call(cfg, z_mla, gq, gkv, wq, wkv, cos, sin):
    H = cfg.mla_heads
    Wz = z_mla.shape[1]
    tm = cfg.C if cfg.C <= 256 else 256
    kb, kr = _key_block(cfg, tm)
    full = lambda a: pl.BlockSpec(a.shape, lambda i: (0,) * a.ndim)
    tab = pl.BlockSpec((tm, LANES), lambda i: (i, 0))
    return pl.pallas_call(
        functools.partial(_mla_prep_kernel, heads=H, q_rank=cfg.q_rank, kv_rank=cfg.kv_rank,
                          qscale=(cfg.nope + cfg.rope) ** -0.5 * LOG2E),
        out_shape=(jax.ShapeDtypeStruct((cfg.T, H * 256), BF16),
                   jax.ShapeDtypeStruct((cfg.B, cfg.Sk, H * 256), BF16),
                   jax.ShapeDtypeStruct((cfg.B, cfg.Sk, H * 256), BF16)),
        grid=(cfg.T // tm,),
        in_specs=[pl.BlockSpec((tm, Wz), lambda i: (i, 0)), full(gq), full(gkv), full(wq), full(wkv), tab, tab],
        out_specs=(pl.BlockSpec((tm, H * 256), lambda i: (i, 0)),
                   pl.BlockSpec((None, tm, H * 256), lambda i: (kb(i), kr(i), 0)),
                   pl.BlockSpec((None, tm, H * 256), lambda i: (kb(i), kr(i), 0))),
        compiler_params=_params(("arbitrary",), 2 * (wq.size + wkv.size) * 2 / MIB + 24 * tm * H * 256 / MIB + 8),
        name="mla_prep",
    )(z_mla, gq, gkv, wq, wkv, cos, sin)


ATTN_SUB_ROWS = 512


def _online_step(q, k, v, m, l, acc):
    s = lax.dot_general(q, k, NT_DIMS, preferred_element_type=F32)
    m_new = jnp.maximum(m, jnp.max(s, axis=-1, keepdims=True))
    alpha = jnp.exp2(m - m_new)
    p = jnp.exp2(s - m_new)
    l = alpha * l + jnp.sum(p, axis=-1, keepdims=True)
    acc = alpha * acc + jnp.dot(p.astype(BF16), v, preferred_element_type=F32)
    return m_new, l, acc


def _da_attn_kernel(lam_ref, g_ref, q_ref, k_ref, v_ref, o_ref, *, tk, nk, sub, lam_init):
    lv = lam_ref[...]
    lam = (jnp.exp(jnp.sum(lv[0:1] * lv[1:2], axis=-1, keepdims=True))
           - jnp.exp(jnp.sum(lv[2:3] * lv[3:4], axis=-1, keepdims=True)) + lam_init)
    neg = jnp.full((sub, 1), -jnp.inf, F32)
    zero = jnp.zeros((sub, 1), F32)
    acc0 = jnp.zeros((sub, v_ref.shape[1]), F32)
    for q0 in range(0, q_ref.shape[0], sub):
        q1 = q_ref[q0:q0 + sub, :LANES]
        q2 = q_ref[q0:q0 + sub, LANES:]
        m1, l1, a1, m2, l2, a2 = neg, zero, acc0, neg, zero, acc0
        for kk in range(nk):
            rows = slice(kk * tk, (kk + 1) * tk)
            v = v_ref[rows, :]
            m1, l1, a1 = _online_step(q1, k_ref[rows, :LANES], v, m1, l1, a1)
            m2, l2, a2 = _online_step(q2, k_ref[rows, LANES:], v, m2, l2, a2)
        o = a1 / l1 - lam * (a2 / l2)
        o_ref[q0:q0 + sub, :] = (_rms(o, g_ref[...]) * (1.0 - lam_init)).astype(o_ref.dtype)


def _into_mix(kernel, n_in, mix, mix_shape):
    out_shape = jax.ShapeDtypeStruct(mix_shape, BF16)
    if mix is None:
        return kernel, [], (), {}, out_shape

    def with_unused_mix_ref(*refs):
        return kernel(*refs[:n_in], *refs[n_in + 1:])

    return with_unused_mix_ref, [pl.BlockSpec(memory_space=pl.ANY)], (mix,), {n_in: 0}, out_shape


def _da_attn_call(cfg, q, k, v, lam_vec, subln_g, lam_init, mix, mix_shape, *, q_row0, sq, sk):
    H, dv = cfg.da_heads, cfg.da_dv
    tq = _pick(sq, (512, 256, 128))
    sub = min(tq, ATTN_SUB_ROWS)
    tk = _pick(sk, (768, 512, 256, 128))
    nq = sq // tq
    q0 = q_row0 // tq
    vm = 2 * (2 * sk * 256 * 2) / MIB + 16 * sub * tk * 4 / MIB + 12
    kern, mix_spec, mix_arg, alias, out_shape = _into_mix(
        functools.partial(_da_attn_kernel, tk=tk, nk=sk // tk, sub=sub, lam_init=lam_init), 5, mix, mix_shape)
    return pl.pallas_call(
        kern,
        out_shape=out_shape,
        input_output_aliases=alias,
        grid=(cfg.B, H, nq),
        in_specs=[pl.BlockSpec((4, LANES), lambda b, h, i: (0, 0)),
                  pl.BlockSpec((1, dv), lambda b, h, i: (0, 0)),
                  pl.BlockSpec((tq, 256), lambda b, h, i: (q0 + b * nq + i, h)),
                  pl.BlockSpec((None, sk, 256), lambda b, h, i: (b, 0, h)),
                  pl.BlockSpec((None, sk, dv), lambda b, h, i: (b, 0, h))] + mix_spec,
        out_specs=pl.BlockSpec((tq, dv), lambda b, h, i: (q0 + b * nq + i, h)),
        compiler_params=_params(("arbitrary", "arbitrary", "arbitrary"), vm),
        name="diff_attention",
    )(lam_vec, subln_g.reshape(1, dv), q, k, v, *mix_arg)


def _mla_attn_kernel(q_ref, k_ref, v_ref, o_ref, *, tk, nk, sub):
    dv = o_ref.shape[1]
    for q0 in range(0, q_ref.shape[0], sub):
        q = q_ref[q0:q0 + sub, :]
        m, acc = jnp.full((sub, 1), -jnp.inf, F32), jnp.zeros((sub, v_ref.shape[1]), F32)
        for kk in range(nk):
            rows = slice(kk * tk, (kk + 1) * tk)
            s = lax.dot_general(q, k_ref[rows, :], NT_DIMS, preferred_element_type=F32)
            m_new = jnp.maximum(m, jnp.max(s, axis=-1, keepdims=True))
            p = jnp.exp2(s - m_new).astype(BF16)
            acc = jnp.exp2(m - m_new) * acc + jnp.dot(p, v_ref[rows, :], preferred_element_type=F32)
            m = m_new
        o_ref[q0:q0 + sub, :] = (acc[:, :dv] / acc[:, dv:dv + 1]).astype(o_ref.dtype)


def _mla_attn_call(cfg, q, k, v, mix, mix_shape, *, q_row0, sq, sk):
    H, dv = cfg.mla_heads, cfg.mla_dv
    col0 = cfg.da_w // dv
    tq = _pick(sq, (1024, 512, 256, 128))
    sub = min(tq, ATTN_SUB_ROWS)
    tk = _pick(sk, (768, 512, 256, 128))
    nq = sq // tq
    q0 = q_row0 // tq
    vm = 2 * (sk * 512 * 2) / MIB + 6 * sub * tk * 4 / MIB + 16
    kern, mix_spec, mix_arg, alias, out_shape = _into_mix(
        functools.partial(_mla_attn_kernel, tk=tk, nk=sk // tk, sub=sub), 3, mix, mix_shape)
    return pl.pallas_call(
        kern,
        out_shape=out_shape,
        input_output_aliases=alias,
        grid=(cfg.B, H, nq),
        in_specs=[pl.BlockSpec((tq, 256), lambda b, h, i: (q0 + b * nq + i, h)),
                  pl.BlockSpec((None, sk, 256), lambda b, h, i: (b, 0, h)),
                  pl.BlockSpec((None, sk, 2 * dv), lambda b, h, i: (b, 0, h))] + mix_spec,
        out_specs=pl.BlockSpec((tq, dv), lambda b, h, i: (q0 + b * nq + i, col0 + h)),
        compiler_params=_params(("arbitrary", "arbitrary", "arbitrary"), vm),
        name="mla_attention",
    )(q, k, v, *mix_arg)


def _chunk_mlp_kernel(z_ref, g_ref, ws_ref, b_ref, o_ref, *, groups, chunk):
    uv = _gelu(z_ref[...].astype(F32))
    w = groups * LANES
    u = uv[:, :w]
    v = _rms(uv[:, w:], g_ref[...]).astype(BF16)
    for c in range(z_ref.shape[0] // chunk):
        rows = slice(c * chunk, (c + 1) * chunk)
        for g in range(groups):
            cols = slice(g * LANES, (g + 1) * LANES)
            y = jnp.dot(ws_ref[g], v[rows, cols], preferred_element_type=F32) + b_ref[:, cols]
            o_ref[rows, cols] = (u[rows, cols] * y).astype(o_ref.dtype)


def _chunk_mlp_call(cfg, z_cm, g, ws, b_full, mix, mix_shape, rows):
    W = cfg.cm_w
    col0 = (cfg.da_w + cfg.mla_w) // W
    tm = _pick(math.gcd(cfg.S, cfg.Tc), (256, 128))
    kern, mix_spec, mix_arg, alias, out_shape = _into_mix(
        functools.partial(_chunk_mlp_kernel, groups=cfg.cm_groups, chunk=cfg.cm_chunk), 4, mix, mix_shape)
    return pl.pallas_call(
        kern,
        out_shape=out_shape,
        input_output_aliases=alias,
        grid=(rows // tm,),
        in_specs=[pl.BlockSpec((tm, 2 * W), lambda i: (i, 0)),
                  pl.BlockSpec((1, W), lambda i: (0, 0)),
                  pl.BlockSpec(ws.shape, lambda i: (0, 0, 0)),
                  pl.BlockSpec(b_full.shape, lambda i: (0, 0))] + mix_spec,
        out_specs=pl.BlockSpec((tm, W), lambda i: (i, col0)),
        compiler_params=_params(("arbitrary",), 40 * tm * W / MIB + 8),
        name="chunk_mlp",
    )(z_cm, g.reshape(1, W), ws, b_full, *mix_arg)


INT32_MIN = -2 ** 31


def _order_key(x):
    b = lax.bitcast_convert_type(x, jnp.int32)
    return jnp.where(b < 0, b ^ 0x7FFFFFFF, b)


def _order_key_inv(k):
    return lax.bitcast_convert_type(jnp.where(k < 0, k ^ 0x7FFFFFFF, k), F32)


def _top_k_sublanes(keys, pos, topk):
    iota_k = lax.broadcasted_iota(jnp.int32, (topk, keys[0].shape[1]), 0)

    def body(k, carry):
        out = []
        for (work, vals), p in zip(carry, pos):
            m = jnp.max(work, axis=0, keepdims=True)
            idx = jnp.min(jnp.where(work == m, p, jnp.inf), axis=0, keepdims=True)
            out.append((jnp.where(p == idx, INT32_MIN + k, work), jnp.where(iota_k == k, m, vals)))
        return tuple(out)

    init = tuple((w, jnp.zeros((topk, w.shape[1]), jnp.int32)) for w in keys)
    res = lax.fori_loop(0, topk, body, init)
    return [(jnp.where(w < INT32_MIN + topk, w - INT32_MIN, topk), v) for w, v in res]


def _pair_blocks(topk):
    return [(0, topk)] + [(a, 8) for a in range(1, 8)]


def _route_kernel(q_ref, keys_ref, er_ref, cnt_ref, ec_ref, rk_ref, *, topk):
    half = q_ref.shape[1] // 2
    tm = q_ref.shape[0]
    s_row = lax.dot_general(keys_ref[0], q_ref[:, :half], NT_DIMS, preferred_element_type=F32)
    s_col = lax.dot_general(keys_ref[1], q_ref[:, half:], NT_DIMS, preferred_element_type=F32)
    iota = lax.broadcasted_iota(jnp.int32, (s_row.shape[0], LANES), 0).astype(F32)
    ranks_r, ranks_c, tops_r, tops_c = [], [], [], []
    for g in range(tm // LANES):
        lanes = slice(g * LANES, (g + 1) * LANES)
        (rr, vr), (rc, vc) = _top_k_sublanes([_order_key(s_row[:, lanes]), _order_key(s_col[:, lanes])],
                                             [iota, iota], topk)
        ranks_r.append(rr)
        ranks_c.append(rc)
        tops_r.append(_order_key_inv(vr))
        tops_c.append(_order_key_inv(vc))
    rank_r = jnp.concatenate(ranks_r, axis=1)
    rank_c = jnp.concatenate(ranks_c, axis=1)
    top_r = jnp.concatenate(tops_r, axis=1)
    top_c = jnp.concatenate(tops_c, axis=1)

    blocks = _pair_blocks(topk)
    cand_parts, pos_parts, valid_parts = [], [], []
    for a, nb in blocks:
        b_iota = lax.broadcasted_iota(jnp.int32, (nb, tm), 0)
        cand_parts.append(top_r[a:a + 1, :] + top_c[:nb, :])
        pos_parts.append((a * topk + b_iota).astype(F32))
        valid_parts.append((a + 1) * (b_iota + 1) <= topk)
    a_iota = lax.broadcasted_iota(jnp.int32, (topk - 8, tm), 0) + 8
    cand_parts.append(top_r[8:, :] + top_c[0:1, :])
    pos_parts.append((a_iota * topk).astype(F32))
    valid_parts.append(a_iota < topk)
    cand = jnp.concatenate(cand_parts, axis=0)
    pos = jnp.concatenate(pos_parts, axis=0)
    valid = jnp.concatenate(valid_parts, axis=0)
    cand_key = jnp.where(valid, _order_key(cand), INT32_MIN + topk)
    (rank_p, best), = _top_k_sublanes([cand_key], [pos], topk)
    chosen = rank_p < topk
    best0 = _order_key_inv(best[0:1, :])
    z = jnp.sum(jnp.where(chosen, jnp.exp(cand - best0), 0.0), axis=0, keepdims=True)
    chosen_f = chosen.astype(F32)
    cnt = jnp.zeros(s_row.shape, F32)
    row0 = 0
    for a, nb in blocks:
        cnt_a = jnp.sum(chosen_f[row0:row0 + nb, :], axis=0, keepdims=True)
        cnt = jnp.where(rank_r == a, cnt_a, cnt)
        row0 += nb
    for a in range(8, topk):
        cnt = jnp.where(rank_r == a, chosen_f[row0 + a - 8:row0 + a - 7, :], cnt)
    er_ref[...] = jnp.where(rank_r < topk, jnp.exp(s_row - top_r[0:1, :]), 0.0)
    cnt_ref[...] = cnt
    ec_ref[...] = jnp.where(rank_c < topk, jnp.exp(s_col - top_c[0:1, :]) / z, 0.0)
    rk_ref[...] = rank_c.astype(F32)


def _route_call(cfg, qp, keys, rows):
    H, nk = cfg.peer_heads, cfg.peer_keys
    tm = _pick(rows, (256, 128))
    out = lambda dt: jax.ShapeDtypeStruct((H, nk, rows), dt)
    ospec = pl.BlockSpec((None, nk, tm), lambda i, h: (h, 0, i))
    return pl.pallas_call(
        functools.partial(_route_kernel, topk=cfg.topk),
        out_shape=(out(F32), out(F32), out(F32), out(F32)),
        grid=(rows // tm, H),
        in_specs=[pl.BlockSpec((tm, cfg.peer_dkey), lambda i, h: (i, h)),
                  pl.BlockSpec((None, 2, nk, cfg.peer_dkey // 2), lambda i, h: (h, 0, 0, 0))],
        out_specs=(ospec, ospec, ospec, ospec),
        compiler_params=_params(("arbitrary", "arbitrary"), 32),
        name="peer_route",
    )(qp, keys)


def _peer_act_kernel(u_ref, h_ref, o_ref):
    o_ref[...] = _gelu(lax.dot_general(u_ref[...], h_ref[...], NT_DIMS, preferred_element_type=F32)).astype(o_ref.dtype)


def _peer_act_call(cfg, u, h2, rows):
    D = cfg.D
    tn, tm = _mm_tiles(cfg.n_exp, D, rows, 2)
    return pl.pallas_call(
        _peer_act_kernel,
        out_shape=jax.ShapeDtypeStruct((cfg.n_exp, rows), BF16),
        grid=(rows // tm, cfg.n_exp // tn),
        in_specs=[pl.BlockSpec((tn, D), lambda i, n: (n, 0)),
                  pl.BlockSpec((tm, D), lambda i, n: (i, 0))],
        out_specs=pl.BlockSpec((tn, tm), lambda i, n: (n, i)),
        compiler_params=_params(("arbitrary", "arbitrary"), _mm_vmem_mib(tn, tm, D, 2) + 4 * tn * tm * 4 / MIB),
        name="peer_activations",
    )(u, h2)


def _peer_out_kernel(gt_ref, er_ref, cnt_ref, ec_ref, rk_ref, v_ref, o_ref, *, heads):
    @pl.when(pl.program_id(1) == 0)
    def _():
        o_ref[...] = jnp.zeros_like(o_ref)

    tk, tm = gt_ref.shape
    R = tk // LANES
    w = jnp.zeros((R, LANES, tm), F32)
    for h in range(heads):
        hit = rk_ref[h][None, :, :] < cnt_ref[h][:, None, :]
        w = w + jnp.where(hit, ec_ref[h][None, :, :], 0.0) * er_ref[h][:, None, :]
    pt = (w.reshape(tk, tm) * gt_ref[...].astype(F32)).astype(BF16)
    o_ref[...] += lax.dot_general(pt, v_ref[...], TN_DIMS, preferred_element_type=F32)


def _peer_out_call(cfg, gt, er, cnt, ec, rk, v, rows):
    H, nk, D = cfg.peer_heads, cfg.peer_keys, cfg.D
    tm = _pick(rows, (512, 256, 128))
    tk = 1024
    R = tk // LANES
    row_spec = pl.BlockSpec((H, R, tm), lambda i, k: (0, k, i))
    col_spec = pl.BlockSpec((H, nk, tm), lambda i, k: (0, 0, i))
    vm = 2 * (tk * tm * 2 + tk * D * 2 + tm * D * 4 + 2 * H * nk * tm * 4) / MIB + 6 * tk * tm * 4 / MIB + 4
    return pl.pallas_call(
        functools.partial(_peer_out_kernel, heads=H),
        out_shape=jax.ShapeDtypeStruct((rows, D), F32),
        grid=(rows // tm, cfg.n_exp // tk),
        in_specs=[pl.BlockSpec((tk, tm), lambda i, k: (k, i)), row_spec, row_spec, col_spec, col_spec,
                  pl.BlockSpec((tk, D), lambda i, k: (k, 0))],
        out_specs=pl.BlockSpec((tm, D), lambda i, k: (i, 0)),
        compiler_params=_params(("arbitrary", "arbitrary"), vm),
        name="peer_output",
    )(gt, er, cnt, ec, rk, v)


def _final_norm_kernel(x_ref, r_ref, gt_ref, g_ref, o_ref):
    o_ref[...] = _rms(x_ref[...] + gt_ref[...] * r_ref[...], g_ref[...])


def _final_norm_call(cfg, x, raw, gate_mods, k_gate, g):
    D = cfg.D
    tm = _pick(math.gcd(cfg.S, cfg.Tc), (256, 128))
    seg = _seg_of_tile(cfg, tm)
    return pl.pallas_call(
        _final_norm_kernel,
        out_shape=jax.ShapeDtypeStruct((cfg.Tx, D), F32),
        grid=(cfg.Tx // tm,),
        in_specs=[pl.BlockSpec((tm, D), lambda i: (i, 0)), pl.BlockSpec((tm, D), lambda i: (i, 0)),
                  pl.BlockSpec((None, None, 1, D), lambda i: (seg(i), k_gate, 0, 0)),
                  pl.BlockSpec((1, D), lambda i: (0, 0))],
        out_specs=pl.BlockSpec((tm, D), lambda i: (i, 0)),
        compiler_params=_params(("arbitrary",), 8 * tm * D * 4 / MIB + 4),
        name="final_norm",
    )(x, raw, gate_mods, g.reshape(1, D))


def _rope_tables(cfg, half, pad):
    pos = jnp.arange(cfg.S, dtype=jnp.int32)
    inv = ROPE_BASE ** (-jnp.arange(half, dtype=F32) / half)
    ang_r = (pos // cfg.grid_w).astype(F32)[:, None] * inv[None, :]
    ang_c = (pos % cfg.grid_w).astype(F32)[:, None] * inv[None, :]
    cos = jnp.concatenate([jnp.cos(ang_r)] * 2 + [jnp.cos(ang_c)] * 2, axis=-1)
    sin = jnp.concatenate([-jnp.sin(ang_r), jnp.sin(ang_r), -jnp.sin(ang_c), jnp.sin(ang_c)], axis=-1)
    cos = jnp.concatenate([jnp.tile(cos, (cfg.B, 1)), jnp.ones((cfg.Tc, 4 * half), F32)], axis=0)
    sin = jnp.concatenate([jnp.tile(sin, (cfg.B, 1)), jnp.zeros((cfg.Tc, 4 * half), F32)], axis=0)
    if pad:
        cos = jnp.pad(cos, ((0, 0), (0, pad)))
        sin = jnp.pad(sin, ((0, 0), (0, pad)))
    return cos, sin


def _forward(cfg, x, c, ctx, c_ctx, w_mod, b_mod, norm1_g, norm2_g, w_in, da_lambda, da_subln_g,
             mla_q_norm_g, mla_kv_norm_g, mla_w_uq, mla_w_ukv, cm_v_norm_g, cm_w_s, cm_b_s,
             w_out, peer_w_q, peer_sub_keys, peer_u, peer_v, final_norm_g):
    B, S, C, D, L = cfg.B, cfg.S, cfg.C, cfg.D, cfg.L
    assert cfg.da_dk == LANES and cfg.nope == LANES and cfg.peer_keys == LANES and cfg.topk == 16
    assert C % LANES == 0 and S % C == 0 and cfg.q_rank % LANES == 0 and cfg.kv_rank % LANES == 0
    assert (cfg.da_w + cfg.mla_w) % cfg.cm_w == 0

    xs = jnp.concatenate([x.reshape(B * S, D), ctx.reshape(B * C, D)], axis=0)
    c_all = jnp.concatenate([c, c_ctx[None, :], jnp.zeros((8 - B - 1, D), F32)], axis=0)
    mods_all = _mod_call(cfg, c_all, w_mod, b_mod).reshape(L, 8, 6, 1, D)
    cos_da, sin_da = _rope_tables(cfg, 32, 0)
    cos_ml, sin_ml = _rope_tables(cfg, 16, LANES - cfg.rope)

    for l in range(L):
        last = l == L - 1
        rows = cfg.Tx if last else cfg.T
        mods = mods_all[l]
        lam_init = 0.8 - 0.6 * math.exp(-0.3 * l)

        w_in_l = w_in[l]
        w_da = w_in_l[:, :cfg.off_cq].astype(BF16)
        ml_cols = cfg.off_cm - cfg.off_cq
        w_ml = jnp.pad(w_in_l[:, cfg.off_cq:cfg.off_cm], ((0, 0), (0, -ml_cols % 256))).astype(BF16)
        w_cm = w_in_l[:, cfg.off_cm:].astype(BF16)
        H = cfg.mla_heads
        wq = jnp.pad(mla_w_uq[l].reshape(cfg.q_rank, H, cfg.nope + cfg.rope),
                     ((0, 0), (0, 0), (0, LANES - cfg.rope))).reshape(cfg.q_rank, H * 256).astype(BF16)
        wkv = mla_w_ukv[l].astype(BF16)
        b_full = jnp.repeat(cm_b_s[l].T, cfg.cm_dg, axis=1)
        keys = peer_sub_keys[l].astype(BF16)

        if l == 0:
            h1 = _norm_mod_call(cfg, xs, norm1_g[l], mods, 0, 1, cfg.T)
        else:
            xs, h1 = _res_norm_mod_call(cfg, xs, peer_raw, mods_all[l - 1], 5, norm1_g[l], mods, 0, 1, cfg.T)
        z_da = _mm_call(h1, w_da, BF16, "in_proj_da")
        z_ml = _mm_call(h1, w_ml, BF16, "in_proj_mla")
        z_cm = _mm_call(h1, w_cm, BF16, "in_proj_cm", rows=rows)
        q_da, k_da, v_da = _da_prep_call(cfg, z_da, cos_da, sin_da)
        q_ml, k_ml, v_ml = _mla_prep_call(cfg, z_ml, mla_q_norm_g[l].reshape(1, -1), mla_kv_norm_g[l].reshape(1, -1),
                                          wq, wkv, cos_ml, sin_ml)
        mshape = (rows, cfg.mix_w)
        da_args = (cfg, q_da, k_da, v_da, da_lambda[l], da_subln_g[l], lam_init)
        mix = _da_attn_call(*da_args, None, mshape, q_row0=0, sq=S, sk=cfg.Sk)
        mix = _mla_attn_call(cfg, q_ml, k_ml, v_ml, mix, mshape, q_row0=0, sq=S, sk=cfg.Sk)
        if not last:
            mix = _da_attn_call(*da_args, mix, mshape, q_row0=cfg.Tx, sq=C, sk=C)
            mix = _mla_attn_call(cfg, q_ml, k_ml, v_ml, mix, mshape, q_row0=cfg.Tx, sq=C, sk=C)
        mix = _chunk_mlp_call(cfg, z_cm, cm_v_norm_g[l], cm_w_s[l].astype(BF16), b_full, mix, mshape, rows)
        xs = _mm_res_call(cfg, mix, _cast_call(w_out, l), xs, mods, 2, rows, "out_proj")

        h2 = _norm_mod_call(cfg, xs, norm2_g[l], mods, 3, 4, rows)
        qp = _mm_call(h2, _cast_call(peer_w_q, l), BF16, "peer_query")
        er, cnt, ec, rk = _route_call(cfg, qp, keys, rows)
        gt = _peer_act_call(cfg, _cast_call(peer_u, l), h2, rows)
        peer_raw = _peer_out_call(cfg, gt, er, cnt, ec, rk, _cast_call(peer_v, l), rows)

    return _final_norm_call(cfg, xs, peer_raw, mods_all[L - 1], 5, final_norm_g).reshape(B, S, D)


def kernel(x, c, ctx, c_ctx, w_mod, b_mod, norm1_g, norm2_g, w_in, da_lambda, da_subln_g, mla_q_norm_g,
           mla_kv_norm_g, mla_w_uq, mla_w_ukv, cm_v_norm_g, cm_w_s, cm_b_s, w_out, peer_w_q, peer_sub_keys,
           peer_u, peer_v, final_norm_g):
    return _forward(_Cfg(), x, c, ctx, c_ctx, w_mod, b_mod, norm1_g, norm2_g, w_in, da_lambda, da_subln_g,
                    mla_q_norm_g, mla_kv_norm_g, mla_w_uq, mla_w_ukv, cm_v_norm_g, cm_w_s, cm_b_s, w_out,
                    peer_w_q, peer_sub_keys, peer_u, peer_v, final_norm_g)
```

```python
import functools
import math

import jax
import jax.numpy as jnp
from jax import lax
from jax.experimental import pallas as pl
from jax.experimental.pallas import tpu as pltpu

F32 = jnp.float32
BF16 = jnp.bfloat16
EPS = 1e-6
ROPE_BASE = 10000.0
LANES = 128
MIB = 1024 * 1024
LOG2E = math.log2(math.e)
NT_DIMS = (((1,), (1,)), ((), ()))
TN_DIMS = (((0,), (0,)), ((), ()))


class _Cfg:
    def __init__(self, d_model=4096, batch=2, seq=8192, depth=2, grid_w=64, ctx_len=256,
                 da_heads=6, mla_heads=12, mla_q_rank=768, mla_kv_rank=512,
                 cm_groups=8, peer_heads=8):
        self.D, self.B, self.S, self.L = d_model, batch, seq, depth
        self.grid_w, self.C = grid_w, ctx_len
        self.da_heads, self.da_dk, self.da_dv = da_heads, 128, 256
        self.mla_heads, self.q_rank, self.kv_rank = mla_heads, mla_q_rank, mla_kv_rank
        self.nope, self.rope, self.mla_dv = 128, 64, 128
        self.cm_groups, self.cm_chunk, self.cm_dg = cm_groups, 128, 128
        self.peer_heads, self.peer_keys, self.peer_dkey, self.topk = peer_heads, 128, 256, 16
        self.da_w = da_heads * self.da_dv
        self.mla_w = mla_heads * self.mla_dv
        self.cm_w = cm_groups * self.cm_dg
        self.mix_w = self.da_w + self.mla_w + self.cm_w
        self.off_da_k = da_heads * 2 * self.da_dk
        self.off_da_v = 2 * self.off_da_k
        self.off_cq = self.off_da_v + self.da_w
        self.off_ckv = self.off_cq + mla_q_rank
        self.off_kr = self.off_ckv + mla_kv_rank
        self.off_cm = self.off_kr + self.rope
        self.in_cols = self.off_cm + 2 * self.cm_w
        self.n_exp = self.peer_keys * self.peer_keys
        self.Tx = batch * seq
        self.Tc = batch * ctx_len
        self.T = self.Tx + self.Tc
        self.Sk = ctx_len + seq


def _pick(n, cands):
    for c in cands:
        if n % c == 0:
            return c
    raise ValueError(f"no tile in {cands} divides {n}")


def _params(sem, vmem_mib, flags=None):
    return pltpu.CompilerParams(dimension_semantics=sem, vmem_limit_bytes=int(vmem_mib * MIB), flags=flags)


def _rms(x, g):
    return x * lax.rsqrt(jnp.mean(x * x, axis=-1, keepdims=True) + EPS) * g


def _gelu(x):
    return 0.5 * x * (1.0 + jnp.tanh(math.sqrt(2.0 / math.pi) * (x + 0.044715 * (x * x * x))))


def _mod_kernel(c_ref, w_ref, b_ref, o_ref):
    c = c_ref[...]
    s = (c * jax.nn.sigmoid(c)).astype(BF16)
    o_ref[...] = jnp.dot(s, w_ref[...].astype(BF16), preferred_element_type=F32) + b_ref[...]


def _mod_call(cfg, c_all, w_mod, b_mod):
    L, D = cfg.L, cfg.D
    n = 6 * D
    tn = _pick(n, (512, 256, 128))
    return pl.pallas_call(
        _mod_kernel,
        out_shape=jax.ShapeDtypeStruct((L, 8, n), F32),
        grid=(L, n // tn),
        in_specs=[pl.BlockSpec((8, D), lambda l, j: (0, 0)),
                  pl.BlockSpec((None, D, tn), lambda l, j: (l, 0, j)),
                  pl.BlockSpec((None, 1, tn), lambda l, j: (l, 0, j))],
        out_specs=pl.BlockSpec((None, 8, tn), lambda l, j: (l, 0, j)),
        compiler_params=_params(("arbitrary", "arbitrary"), 2 * D * tn * 4 / MIB + 3 * D * tn * 2 / MIB + 4),
        name="adaln_mod",
    )(c_all, w_mod, b_mod.reshape(L, 1, n))


def _cast_kernel(w_ref, o_ref):
    o_ref[...] = w_ref[...].astype(o_ref.dtype)


def _cast_call(w, l):
    _, R, Cw = w.shape
    tr = _pick(R, (512, 256, 128))
    return pl.pallas_call(
        _cast_kernel,
        out_shape=jax.ShapeDtypeStruct((R, Cw), BF16),
        grid=(R // tr,),
        in_specs=[pl.BlockSpec((None, tr, Cw), lambda i: (l, i, 0))],
        out_specs=pl.BlockSpec((tr, Cw), lambda i: (i, 0)),
        compiler_params=_params(("arbitrary",), 2 * tr * Cw * 6 / MIB + 4),
        name="weight_cast",
    )(w)


def _norm_mod_kernel(x_ref, g_ref, sh_ref, sc_ref, o_ref):
    y = _rms(x_ref[...], g_ref[...])
    o_ref[...] = (y * (1.0 + sc_ref[...]) + sh_ref[...]).astype(o_ref.dtype)


def _res_norm_mod_kernel(x_ref, r_ref, gt_ref, g_ref, sh_ref, sc_ref, xo_ref, o_ref):
    x = x_ref[...] + gt_ref[...] * r_ref[...]
    xo_ref[...] = x
    o_ref[...] = (_rms(x, g_ref[...]) * (1.0 + sc_ref[...]) + sh_ref[...]).astype(o_ref.dtype)


def _seg_of_tile(cfg, tm):
    return lambda i: jnp.minimum((i * tm) // cfg.S, cfg.B)


def _res_norm_mod_call(cfg, x, raw, gate_mods, k_gate, g, mods, k_shift, k_scale, rows):
    D = cfg.D
    tm = _pick(math.gcd(cfg.S, cfg.Tc), (256, 128))
    seg = _seg_of_tile(cfg, tm)
    row = pl.BlockSpec((tm, D), lambda i: (i, 0))
    vec = lambda k: pl.BlockSpec((None, None, 1, D), lambda i: (seg(i), k, 0, 0))
    return pl.pallas_call(
        _res_norm_mod_kernel,
        out_shape=(jax.ShapeDtypeStruct((rows, D), F32), jax.ShapeDtypeStruct((rows, D), BF16)),
        grid=(rows // tm,),
        in_specs=[row, row, vec(k_gate), pl.BlockSpec((1, D), lambda i: (0, 0)), vec(k_shift), vec(k_scale)],
        out_specs=(row, row),
        compiler_params=_params(("arbitrary",), 12 * tm * D * 4 / MIB + 4),
        name="residual_norm_modulate",
    )(x, raw, gate_mods, g.reshape(1, D), mods, mods)


def _norm_mod_call(cfg, x, g, mods, k_shift, k_scale, rows):
    D = cfg.D
    tm = _pick(math.gcd(cfg.S, cfg.Tc), (512, 256, 128))
    seg = _seg_of_tile(cfg, tm)
    return pl.pallas_call(
        _norm_mod_kernel,
        out_shape=jax.ShapeDtypeStruct((rows, D), BF16),
        grid=(rows // tm,),
        in_specs=[pl.BlockSpec((tm, D), lambda i: (i, 0)),
                  pl.BlockSpec((1, D), lambda i: (0, 0)),
                  pl.BlockSpec((None, None, 1, D), lambda i: (seg(i), k_shift, 0, 0)),
                  pl.BlockSpec((None, None, 1, D), lambda i: (seg(i), k_scale, 0, 0))],
        out_specs=pl.BlockSpec((tm, D), lambda i: (i, 0)),
        compiler_params=_params(("arbitrary",), 5 * tm * D * 4 / MIB + 4),
        name="norm_modulate",
    )(x, g.reshape(1, D), mods, mods)


def _mm_kernel(a_ref, b_ref, o_ref):
    o_ref[...] = jnp.dot(a_ref[...], b_ref[...], preferred_element_type=F32).astype(o_ref.dtype)


MM_VMEM_BUDGET_MIB = 48


def _mm_vmem_mib(tm, tn, K, osz):
    return (2 * (tm * K * 2 + K * tn * 2 + tm * tn * osz) + tm * tn * 4) / MIB + 4


def _mm_tiles(M, K, N, osz):
    fits = [(tm * tn / (tm + tn), tn, tm)
            for tm in (1536, 1024, 512, 256, 128) for tn in (N, 1536, 1024, 768, 512, 256, 128)
            if M % tm == 0 and N % tn == 0 and (tn == N or tn % LANES == 0)
            and _mm_vmem_mib(tm, tn, K, osz) <= MM_VMEM_BUDGET_MIB]
    if not fits:
        raise ValueError(f"no matmul tiling for {(M, K, N)}")
    _, tn, tm = max(fits)
    return tm, tn


def _mm_call(a, b, out_dtype, name, rows=None):
    M = a.shape[0] if rows is None else rows
    K, N = b.shape
    osz = jnp.dtype(out_dtype).itemsize
    tm, tn = _mm_tiles(M, K, N, osz)
    vm = _mm_vmem_mib(tm, tn, K, osz)
    return pl.pallas_call(
        _mm_kernel,
        out_shape=jax.ShapeDtypeStruct((M, N), out_dtype),
        grid=(M // tm, N // tn),
        in_specs=[pl.BlockSpec((tm, K), lambda i, j: (i, 0)),
                  pl.BlockSpec((K, tn), lambda i, j: (0, j))],
        out_specs=pl.BlockSpec((tm, tn), lambda i, j: (i, j)),
        compiler_params=_params(("arbitrary", "arbitrary"), vm),
        name=name,
    )(a, b)


def _mm_res_kernel(a_ref, b_ref, x_ref, g_ref, o_ref):
    acc = jnp.dot(a_ref[...], b_ref[...], preferred_element_type=F32)
    o_ref[...] = x_ref[...] + g_ref[...] * acc


def _mm_res_call(cfg, a, b, x, mods, k_gate, rows, name):
    K, N = b.shape
    tm = _pick(math.gcd(cfg.S, cfg.Tc), (512, 256, 128))
    tn = _pick(N, (1024, 512, 256, 128))
    seg = _seg_of_tile(cfg, tm)
    vm = 2 * (tm * K * 2 + K * tn * 2 + 2 * tm * tn * 4) / MIB + tm * tn * 4 / MIB + 4
    return pl.pallas_call(
        _mm_res_kernel,
        out_shape=jax.ShapeDtypeStruct((rows, N), F32),
        grid=(rows // tm, N // tn),
        in_specs=[pl.BlockSpec((tm, K), lambda i, j: (i, 0)),
                  pl.BlockSpec((K, tn), lambda i, j: (0, j)),
                  pl.BlockSpec((tm, tn), lambda i, j: (i, j)),
                  pl.BlockSpec((None, None, 1, tn), lambda i, j: (seg(i), k_gate, 0, j))],
        out_specs=pl.BlockSpec((tm, tn), lambda i, j: (i, j)),
        compiler_params=_params(("arbitrary", "arbitrary"), vm),
        name=name,
    )(a, b, x, mods)


def _key_block(cfg, tm):
    n_lat = cfg.Tx // tm
    per_b = cfg.S // tm
    per_c = cfg.C // tm

    def batch(i):
        return jnp.where(i < n_lat, i // per_b, (i - n_lat) // per_c)

    def row(i):
        return jnp.where(i < n_lat, per_c + i % per_b, (i - n_lat) % per_c)

    return batch, row


def _swap_halves(x, lane, half):
    first = (lane % (2 * half)) < half
    return jnp.where(first, pltpu.roll(x, LANES - half, 1), pltpu.roll(x, half, 1))


def _da_prep_kernel(q_ref, k_ref, v_ref, cos_ref, sin_ref, qo_ref, ko_ref, vo_ref, *, groups, qscale):
    cos = cos_ref[...]
    sin = sin_ref[...]
    lane = lax.broadcasted_iota(jnp.int32, cos.shape, 1)

    def rope(x):
        return x * cos + _swap_halves(x, lane, 32) * sin

    for g in range(groups):
        sl = slice(g * LANES, (g + 1) * LANES)
        qo_ref[:, sl] = (rope(q_ref[:, sl].astype(F32)) * qscale).astype(BF16)
        ko_ref[:, sl] = rope(k_ref[:, sl].astype(F32)).astype(BF16)
    vo_ref[...] = v_ref[...]


def _da_prep_call(cfg, z_da, cos, sin):
    W = cfg.da_w
    tm = cfg.C if cfg.C <= 256 else 256
    kb, kr = _key_block(cfg, tm)
    tok = lambda c: pl.BlockSpec((tm, W), lambda i: (i, c))
    tab = pl.BlockSpec((tm, LANES), lambda i: (i, 0))
    key = pl.BlockSpec((None, tm, W), lambda i: (kb(i), kr(i), 0))
    return pl.pallas_call(
        functools.partial(_da_prep_kernel, groups=W // LANES, qscale=cfg.da_dk ** -0.5 * LOG2E),
        out_shape=(jax.ShapeDtypeStruct((cfg.T, W), BF16),
                   jax.ShapeDtypeStruct((cfg.B, cfg.Sk, W), BF16),
                   jax.ShapeDtypeStruct((cfg.B, cfg.Sk, W), BF16)),
        grid=(cfg.T // tm,),
        in_specs=[tok(0), tok(1), tok(2), tab, tab],
        out_specs=(tok(0), key, key),
        compiler_params=_params(("arbitrary",), 12 * tm * W * 2 / MIB + 8),
        name="da_prep",
    )(z_da, z_da, z_da, cos, sin)


def _mla_prep_kernel(z_ref, gq_ref, gkv_ref, wq_ref, wkv_ref, cos_ref, sin_ref,
                     qo_ref, ko_ref, vo_ref, *, heads, q_rank, kv_rank, qscale):
    z = z_ref[...].astype(F32)
    cos = cos_ref[...]
    sin = sin_ref[...]
    lane = lax.broadcasted_iota(jnp.int32, cos.shape, 1)

    def rope(x):
        return x * cos + _swap_halves(x, lane, 16) * sin

    cq = _rms(z[:, :q_rank], gq_ref[...]).astype(BF16)
    ckv = _rms(z[:, q_rank:q_rank + kv_rank], gkv_ref[...]).astype(BF16)
    kpe = rope(z[:, q_rank + kv_rank:q_rank + kv_rank + LANES]).astype(BF16)
    q = jnp.dot(cq, wq_ref[...], preferred_element_type=F32)
    kv = jnp.dot(ckv, wkv_ref[...], preferred_element_type=F32)
    for h in range(heads):
        a, b, c = 2 * h * LANES, (2 * h + 1) * LANES, (2 * h + 2) * LANES
        qo_ref[:, a:b] = (q[:, a:b] * qscale).astype(BF16)
        qo_ref[:, b:c] = (rope(q[:, b:c]) * qscale).astype(BF16)
        ko_ref[:, a:b] = kv[:, a:b].astype(BF16)
        ko_ref[:, b:c] = kpe
        vo_ref[:, a:b] = kv[:, b:c].astype(BF16)
        vo_ref[:, b:c] = jnp.ones((z.shape[0], LANES), BF16)


def _mla_prep_call(cfg, z_mla, gq, gkv, wq, wkv, cos, sin):
    H = cfg.mla_heads
    Wz = z_mla.shape[1]
    tm = cfg.C if cfg.C <= 256 else 256
    kb, kr = _key_block(cfg, tm)
    full = lambda a: pl.BlockSpec(a.shape, lambda i: (0,) * a.ndim)
    tab = pl.BlockSpec((tm, LANES), lambda i: (i, 0))
    return pl.pallas_call(
        functools.partial(_mla_prep_kernel, heads=H, q_rank=cfg.q_rank, kv_rank=cfg.kv_rank,
                          qscale=(cfg.nope + cfg.rope) ** -0.5 * LOG2E),
        out_shape=(jax.ShapeDtypeStruct((cfg.T, H * 256), BF16),
                   jax.ShapeDtypeStruct((cfg.B, cfg.Sk, H * 256), BF16),
                   jax.ShapeDtypeStruct((cfg.B, cfg.Sk, H * 256), BF16)),
        grid=(cfg.T // tm,),
        in_specs=[pl.BlockSpec((tm, Wz), lambda i: (i, 0)), full(gq), full(gkv), full(wq), full(wkv), tab, tab],
        out_specs=(pl.BlockSpec((tm, H * 256), lambda i: (i, 0)),
                   pl.BlockSpec((None, tm, H * 256), lambda i: (kb(i), kr(i), 0)),
                   pl.BlockSpec((None, tm, H * 256), lambda i: (kb(i), kr(i), 0))),
        compiler_params=_params(("arbitrary",), 2 * (wq.size + wkv.size) * 2 / MIB + 24 * tm * H * 256 / MIB + 8),
        name="mla_prep",
    )(z_mla, gq, gkv, wq, wkv, cos, sin)


ATTN_SUB_ROWS = 512


def _online_step(q, k, v, m, l, acc):
    s = lax.dot_general(q, k, NT_DIMS, preferred_element_type=F32)
    m_new = jnp.maximum(m, jnp.max(s, axis=-1, keepdims=True))
    alpha = jnp.exp2(m - m_new)
    p = jnp.exp2(s - m_new)
    l = alpha * l + jnp.sum(p, axis=-1, keepdims=True)
    acc = alpha * acc + jnp.dot(p.astype(BF16), v, preferred_element_type=F32)
    return m_new, l, acc


def _da_attn_kernel(lam_ref, g_ref, q_ref, k_ref, v_ref, o_ref, *, tk, nk, sub, lam_init):
    lv = lam_ref[...]
    lam = (jnp.exp(jnp.sum(lv[0:1] * lv[1:2], axis=-1, keepdims=True))
           - jnp.exp(jnp.sum(lv[2:3] * lv[3:4], axis=-1, keepdims=True)) + lam_init)
    neg = jnp.full((sub, 1), -jnp.inf, F32)
    zero = jnp.zeros((sub, 1), F32)
    acc0 = jnp.zeros((sub, v_ref.shape[1]), F32)
    for q0 in range(0, q_ref.shape[0], sub):
        q1 = q_ref[q0:q0 + sub, :LANES]
        q2 = q_ref[q0:q0 + sub, LANES:]
        m1, l1, a1, m2, l2, a2 = neg, zero, acc0, neg, zero, acc0
        for kk in range(nk):
            rows = slice(kk * tk, (kk + 1) * tk)
            v = v_ref[rows, :]
            m1, l1, a1 = _online_step(q1, k_ref[rows, :LANES], v, m1, l1, a1)
            m2, l2, a2 = _online_step(q2, k_ref[rows, LANES:], v, m2, l2, a2)
        o = a1 / l1 - lam * (a2 / l2)
        o_ref[q0:q0 + sub, :] = (_rms(o, g_ref[...]) * (1.0 - lam_init)).astype(o_ref.dtype)


def _into_mix(kernel, n_in, mix, mix_shape):
    out_shape = jax.ShapeDtypeStruct(mix_shape, BF16)
    if mix is None:
        return kernel, [], (), {}, out_shape

    def with_unused_mix_ref(*refs):
        return kernel(*refs[:n_in], *refs[n_in + 1:])

    return with_unused_mix_ref, [pl.BlockSpec(memory_space=pl.ANY)], (mix,), {n_in: 0}, out_shape


def _da_attn_call(cfg, q, k, v, lam_vec, subln_g, lam_init, mix, mix_shape, *, q_row0, sq, sk):
    H, dv = cfg.da_heads, cfg.da_dv
    tq = _pick(sq, (512, 256, 128))
    sub = min(tq, ATTN_SUB_ROWS)
    tk = _pick(sk, (768, 512, 256, 128))
    nq = sq // tq
    q0 = q_row0 // tq
    vm = 2 * (2 * sk * 256 * 2) / MIB + 16 * sub * tk * 4 / MIB + 12
    kern, mix_spec, mix_arg, alias, out_shape = _into_mix(
        functools.partial(_da_attn_kernel, tk=tk, nk=sk // tk, sub=sub, lam_init=lam_init), 5, mix, mix_shape)
    return pl.pallas_call(
        kern,
        out_shape=out_shape,
        input_output_aliases=alias,
        grid=(cfg.B, H, nq),
        in_specs=[pl.BlockSpec((4, LANES), lambda b, h, i: (0, 0)),
                  pl.BlockSpec((1, dv), lambda b, h, i: (0, 0)),
                  pl.BlockSpec((tq, 256), lambda b, h, i: (q0 + b * nq + i, h)),
                  pl.BlockSpec((None, sk, 256), lambda b, h, i: (b, 0, h)),
                  pl.BlockSpec((None, sk, dv), lambda b, h, i: (b, 0, h))] + mix_spec,
        out_specs=pl.BlockSpec((tq, dv), lambda b, h, i: (q0 + b * nq + i, h)),
        compiler_params=_params(("arbitrary", "arbitrary", "arbitrary"), vm),
        name="diff_attention",
    )(lam_vec, subln_g.reshape(1, dv), q, k, v, *mix_arg)


def _mla_attn_kernel(q_ref, k_ref, v_ref, o_ref, *, tk, nk, sub):
    dv = o_ref.shape[1]
    for q0 in range(0, q_ref.shape[0], sub):
        q = q_ref[q0:q0 + sub, :]
        m, acc = jnp.full((sub, 1), -jnp.inf, F32), jnp.zeros((sub, v_ref.shape[1]), F32)
        for kk in range(nk):
            rows = slice(kk * tk, (kk + 1) * tk)
            s = lax.dot_general(q, k_ref[rows, :], NT_DIMS, preferred_element_type=F32)
            m_new = jnp.maximum(m, jnp.max(s, axis=-1, keepdims=True))
            p = jnp.exp2(s - m_new).astype(BF16)
            acc = jnp.exp2(m - m_new) * acc + jnp.dot(p, v_ref[rows, :], preferred_element_type=F32)
            m = m_new
        o_ref[q0:q0 + sub, :] = (acc[:, :dv] / acc[:, dv:dv + 1]).astype(o_ref.dtype)


def _mla_attn_call(cfg, q, k, v, mix, mix_shape, *, q_row0, sq, sk):
    H, dv = cfg.mla_heads, cfg.mla_dv
    col0 = cfg.da_w // dv
    tq = _pick(sq, (1024, 512, 256, 128))
    sub = min(tq, ATTN_SUB_ROWS)
    tk = _pick(sk, (768, 512, 256, 128))
    nq = sq // tq
    q0 = q_row0 // tq
    vm = 2 * (sk * 512 * 2) / MIB + 6 * sub * tk * 4 / MIB + 16
    kern, mix_spec, mix_arg, alias, out_shape = _into_mix(
        functools.partial(_mla_attn_kernel, tk=tk, nk=sk // tk, sub=sub), 3, mix, mix_shape)
    return pl.pallas_call(
        kern,
        out_shape=out_shape,
        input_output_aliases=alias,
        grid=(cfg.B, H, nq),
        in_specs=[pl.BlockSpec((tq, 256), lambda b, h, i: (q0 + b * nq + i, h)),
                  pl.BlockSpec((None, sk, 256), lambda b, h, i: (b, 0, h)),
                  pl.BlockSpec((None, sk, 2 * dv), lambda b, h, i: (b, 0, h))] + mix_spec,
        out_specs=pl.BlockSpec((tq, dv), lambda b, h, i: (q0 + b * nq + i, col0 + h)),
        compiler_params=_params(("arbitrary", "arbitrary", "arbitrary"), vm),
        name="mla_attention",
    )(q, k, v, *mix_arg)


def _chunk_mlp_kernel(z_ref, g_ref, ws_ref, b_ref, o_ref, *, groups, chunk):
    uv = _gelu(z_ref[...].astype(F32))
    w = groups * LANES
    u = uv[:, :w]
    v = _rms(uv[:, w:], g_ref[...]).astype(BF16)
    for c in range(z_ref.shape[0] // chunk):
        rows = slice(c * chunk, (c + 1) * chunk)
        for g in range(groups):
            cols = slice(g * LANES, (g + 1) * LANES)
            y = jnp.dot(ws_ref[g], v[rows, cols], preferred_element_type=F32) + b_ref[:, cols]
            o_ref[rows, cols] = (u[rows, cols] * y).astype(o_ref.dtype)


def _chunk_mlp_call(cfg, z_cm, g, ws, b_full, mix, mix_shape, rows):
    W = cfg.cm_w
    col0 = (cfg.da_w + cfg.mla_w) // W
    tm = _pick(math.gcd(cfg.S, cfg.Tc), (256, 128))
    kern, mix_spec, mix_arg, alias, out_shape = _into_mix(
        functools.partial(_chunk_mlp_kernel, groups=cfg.cm_groups, chunk=cfg.cm_chunk), 4, mix, mix_shape)
    return pl.pallas_call(
        kern,
        out_shape=out_shape,
        input_output_aliases=alias,
        grid=(rows // tm,),
        in_specs=[pl.BlockSpec((tm, 2 * W), lambda i: (i, 0)),
                  pl.BlockSpec((1, W), lambda i: (0, 0)),
                  pl.BlockSpec(ws.shape, lambda i: (0, 0, 0)),
                  pl.BlockSpec(b_full.shape, lambda i: (0, 0))] + mix_spec,
        out_specs=pl.BlockSpec((tm, W), lambda i: (i, col0)),
        compiler_params=_params(("arbitrary",), 40 * tm * W / MIB + 8),
        name="chunk_mlp",
    )(z_cm, g.reshape(1, W), ws, b_full, *mix_arg)


INT32_MIN = -2 ** 31


def _order_key(x):
    b = lax.bitcast_convert_type(x, jnp.int32)
    return jnp.where(b < 0, b ^ 0x7FFFFFFF, b)


def _order_key_inv(k):
    return lax.bitcast_convert_type(jnp.where(k < 0, k ^ 0x7FFFFFFF, k), F32)


def _top_k_sublanes(keys, pos, topk):
    iota_k = lax.broadcasted_iota(jnp.int32, (topk, keys[0].shape[1]), 0)

    def body(k, carry):
        out = []
        for (work, vals), p in zip(carry, pos):
            m = jnp.max(work, axis=0, keepdims=True)
            idx = jnp.min(jnp.where(work == m, p, jnp.inf), axis=0, keepdims=True)
            out.append((jnp.where(p == idx, INT32_MIN + k, work), jnp.where(iota_k == k, m, vals)))
        return tuple(out)

    init = tuple((w, jnp.zeros((topk, w.shape[1]), jnp.int32)) for w in keys)
    res = lax.fori_loop(0, topk, body, init)
    return [(jnp.where(w < INT32_MIN + topk, w - INT32_MIN, topk), v) for w, v in res]


def _pair_blocks(topk):
    return [(0, topk)] + [(a, 8) for a in range(1, 8)]


def _route_kernel(q_ref, keys_ref, er_ref, cnt_ref, ec_ref, rk_ref, *, topk):
    half = q_ref.shape[1] // 2
    tm = q_ref.shape[0]
    s_row = lax.dot_general(keys_ref[0], q_ref[:, :half], NT_DIMS, preferred_element_type=F32)
    s_col = lax.dot_general(keys_ref[1], q_ref[:, half:], NT_DIMS, preferred_element_type=F32)
    iota = lax.broadcasted_iota(jnp.int32, (s_row.shape[0], LANES), 0).astype(F32)
    groups = [slice(g * LANES, (g + 1) * LANES) for g in range(tm // LANES)]
    res = _top_k_sublanes([_order_key(s[:, lanes]) for lanes in groups for s in (s_row, s_col)],
                          [iota] * (2 * len(groups)), topk)
    rank_r = jnp.concatenate([r for r, _ in res[0::2]], axis=1)
    rank_c = jnp.concatenate([r for r, _ in res[1::2]], axis=1)
    top_r = jnp.concatenate([_order_key_inv(v) for _, v in res[0::2]], axis=1)
    top_c = jnp.concatenate([_order_key_inv(v) for _, v in res[1::2]], axis=1)

    blocks = _pair_blocks(topk)
    cand_parts, pos_parts, valid_parts = [], [], []
    for a, nb in blocks:
        b_iota = lax.broadcasted_iota(jnp.int32, (nb, tm), 0)
        cand_parts.append(top_r[a:a + 1, :] + top_c[:nb, :])
        pos_parts.append((a * topk + b_iota).astype(F32))
        valid_parts.append((a + 1) * (b_iota + 1) <= topk)
    a_iota = lax.broadcasted_iota(jnp.int32, (topk - 8, tm), 0) + 8
    cand_parts.append(top_r[8:, :] + top_c[0:1, :])
    pos_parts.append((a_iota * topk).astype(F32))
    valid_parts.append(a_iota < topk)
    cand = jnp.concatenate(cand_parts, axis=0)
    pos = jnp.concatenate(pos_parts, axis=0)
    valid = jnp.concatenate(valid_parts, axis=0)
    cand_key = jnp.where(valid, _order_key(cand), INT32_MIN + topk)
    (rank_p, best), = _top_k_sublanes([cand_key], [pos], topk)
    chosen = rank_p < topk
    best0 = _order_key_inv(best[0:1, :])
    z = jnp.sum(jnp.where(chosen, jnp.exp(cand - best0), 0.0), axis=0, keepdims=True)
    chosen_f = chosen.astype(F32)
    cnt = jnp.zeros(s_row.shape, F32)
    row0 = 0
    for a, nb in blocks:
        cnt_a = jnp.sum(chosen_f[row0:row0 + nb, :], axis=0, keepdims=True)
        cnt = jnp.where(rank_r == a, cnt_a, cnt)
        row0 += nb
    for a in range(8, topk):
        cnt = jnp.where(rank_r == a, chosen_f[row0 + a - 8:row0 + a - 7, :], cnt)
    er_ref[...] = jnp.where(rank_r < topk, jnp.exp(s_row - top_r[0:1, :]), 0.0)
    cnt_ref[...] = cnt
    ec_ref[...] = jnp.where(rank_c < topk, jnp.exp(s_col - top_c[0:1, :]) / z, 0.0)
    rk_ref[...] = rank_c.astype(F32)


def _route_call(cfg, qp, keys, rows):
    H, nk = cfg.peer_heads, cfg.peer_keys
    tm = _pick(rows, (256, 128))
    out = lambda dt: jax.ShapeDtypeStruct((H, nk, rows), dt)
    ospec = pl.BlockSpec((None, nk, tm), lambda i, h: (h, 0, i))
    return pl.pallas_call(
        functools.partial(_route_kernel, topk=cfg.topk),
        out_shape=(out(F32), out(F32), out(F32), out(F32)),
        grid=(rows // tm, H),
        in_specs=[pl.BlockSpec((tm, cfg.peer_dkey), lambda i, h: (i, h)),
                  pl.BlockSpec((None, 2, nk, cfg.peer_dkey // 2), lambda i, h: (h, 0, 0, 0))],
        out_specs=(ospec, ospec, ospec, ospec),
        compiler_params=_params(("arbitrary", "arbitrary"), 32),
        name="peer_route",
    )(qp, keys)


def _peer_act_kernel(u_ref, h_ref, o_ref):
    o_ref[...] = _gelu(lax.dot_general(u_ref[...], h_ref[...], NT_DIMS, preferred_element_type=F32)).astype(o_ref.dtype)


def _peer_act_call(cfg, u, h2, rows):
    D = cfg.D
    tn, tm = _mm_tiles(cfg.n_exp, D, rows, 2)
    return pl.pallas_call(
        _peer_act_kernel,
        out_shape=jax.ShapeDtypeStruct((cfg.n_exp, rows), BF16),
        grid=(rows // tm, cfg.n_exp // tn),
        in_specs=[pl.BlockSpec((tn, D), lambda i, n: (n, 0)),
                  pl.BlockSpec((tm, D), lambda i, n: (i, 0))],
        out_specs=pl.BlockSpec((tn, tm), lambda i, n: (n, i)),
        compiler_params=_params(("arbitrary", "arbitrary"), _mm_vmem_mib(tn, tm, D, 2) + 4 * tn * tm * 4 / MIB),
        name="peer_activations",
    )(u, h2)


def _peer_out_kernel(gt_ref, er_ref, cnt_ref, ec_ref, rk_ref, v_ref, o_ref, *, heads):
    @pl.when(pl.program_id(1) == 0)
    def _():
        o_ref[...] = jnp.zeros_like(o_ref)

    tk, tm = gt_ref.shape
    R = tk // LANES
    w = jnp.zeros((R, LANES, tm), F32)
    for h in range(heads):
        hit = rk_ref[h][None, :, :] < cnt_ref[h][:, None, :]
        w = w + jnp.where(hit, ec_ref[h][None, :, :], 0.0) * er_ref[h][:, None, :]
    pt = (w.reshape(tk, tm) * gt_ref[...].astype(F32)).astype(BF16)
    o_ref[...] += lax.dot_general(pt, v_ref[...], TN_DIMS, preferred_element_type=F32)


def _peer_out_call(cfg, gt, er, cnt, ec, rk, v, rows):
    H, nk, D = cfg.peer_heads, cfg.peer_keys, cfg.D
    tm = _pick(rows, (512, 256, 128))
    tk = 1024
    R = tk // LANES
    row_spec = pl.BlockSpec((H, R, tm), lambda i, k: (0, k, i))
    col_spec = pl.BlockSpec((H, nk, tm), lambda i, k: (0, 0, i))
    vm = 2 * (tk * tm * 2 + tk * D * 2 + tm * D * 4 + 2 * H * nk * tm * 4) / MIB + 6 * tk * tm * 4 / MIB + 4
    return pl.pallas_call(
        functools.partial(_peer_out_kernel, heads=H),
        out_shape=jax.ShapeDtypeStruct((rows, D), F32),
        grid=(rows // tm, cfg.n_exp // tk),
        in_specs=[pl.BlockSpec((tk, tm), lambda i, k: (k, i)), row_spec, row_spec, col_spec, col_spec,
                  pl.BlockSpec((tk, D), lambda i, k: (k, 0))],
        out_specs=pl.BlockSpec((tm, D), lambda i, k: (i, 0)),
        compiler_params=_params(("arbitrary", "arbitrary"), vm),
        name="peer_output",
    )(gt, er, cnt, ec, rk, v)


def _final_norm_kernel(x_ref, r_ref, gt_ref, g_ref, o_ref):
    o_ref[...] = _rms(x_ref[...] + gt_ref[...] * r_ref[...], g_ref[...])


def _final_norm_call(cfg, x, raw, gate_mods, k_gate, g):
    D = cfg.D
    tm = _pick(math.gcd(cfg.S, cfg.Tc), (256, 128))
    seg = _seg_of_tile(cfg, tm)
    return pl.pallas_call(
        _final_norm_kernel,
        out_shape=jax.ShapeDtypeStruct((cfg.Tx, D), F32),
        grid=(cfg.Tx // tm,),
        in_specs=[pl.BlockSpec((tm, D), lambda i: (i, 0)), pl.BlockSpec((tm, D), lambda i: (i, 0)),
                  pl.BlockSpec((None, None, 1, D), lambda i: (seg(i), k_gate, 0, 0)),
                  pl.BlockSpec((1, D), lambda i: (0, 0))],
        out_specs=pl.BlockSpec((tm, D), lambda i: (i, 0)),
        compiler_params=_params(("arbitrary",), 8 * tm * D * 4 / MIB + 4),
        name="final_norm",
    )(x, raw, gate_mods, g.reshape(1, D))


def _rope_tables(cfg, half, pad):
    pos = jnp.arange(cfg.S, dtype=jnp.int32)
    inv = ROPE_BASE ** (-jnp.arange(half, dtype=F32) / half)
    ang_r = (pos // cfg.grid_w).astype(F32)[:, None] * inv[None, :]
    ang_c = (pos % cfg.grid_w).astype(F32)[:, None] * inv[None, :]
    cos = jnp.concatenate([jnp.cos(ang_r)] * 2 + [jnp.cos(ang_c)] * 2, axis=-1)
    sin = jnp.concatenate([-jnp.sin(ang_r), jnp.sin(ang_r), -jnp.sin(ang_c), jnp.sin(ang_c)], axis=-1)
    cos = jnp.concatenate([jnp.tile(cos, (cfg.B, 1)), jnp.ones((cfg.Tc, 4 * half), F32)], axis=0)
    sin = jnp.concatenate([jnp.tile(sin, (cfg.B, 1)), jnp.zeros((cfg.Tc, 4 * half), F32)], axis=0)
    if pad:
        cos = jnp.pad(cos, ((0, 0), (0, pad)))
        sin = jnp.pad(sin, ((0, 0), (0, pad)))
    return cos, sin


def _forward(cfg, x, c, ctx, c_ctx, w_mod, b_mod, norm1_g, norm2_g, w_in, da_lambda, da_subln_g,
             mla_q_norm_g, mla_kv_norm_g, mla_w_uq, mla_w_ukv, cm_v_norm_g, cm_w_s, cm_b_s,
             w_out, peer_w_q, peer_sub_keys, peer_u, peer_v, final_norm_g):
    B, S, C, D, L = cfg.B, cfg.S, cfg.C, cfg.D, cfg.L
    assert cfg.da_dk == LANES and cfg.nope == LANES and cfg.peer_keys == LANES and cfg.topk == 16
    assert C % LANES == 0 and S % C == 0 and cfg.q_rank % LANES == 0 and cfg.kv_rank % LANES == 0
    assert (cfg.da_w + cfg.mla_w) % cfg.cm_w == 0

    xs = jnp.concatenate([x.reshape(B * S, D), ctx.reshape(B * C, D)], axis=0)
    c_all = jnp.concatenate([c, c_ctx[None, :], jnp.zeros((8 - B - 1, D), F32)], axis=0)
    mods_all = _mod_call(cfg, c_all, w_mod, b_mod).reshape(L, 8, 6, 1, D)
    cos_da, sin_da = _rope_tables(cfg, 32, 0)
    cos_ml, sin_ml = _rope_tables(cfg, 16, LANES - cfg.rope)

    for l in range(L):
        last = l == L - 1
        rows = cfg.Tx if last else cfg.T
        mods = mods_all[l]
        lam_init = 0.8 - 0.6 * math.exp(-0.3 * l)

        w_in_l = w_in[l]
        w_da = w_in_l[:, :cfg.off_cq].astype(BF16)
        ml_cols = cfg.off_cm - cfg.off_cq
        w_ml = jnp.pad(w_in_l[:, cfg.off_cq:cfg.off_cm], ((0, 0), (0, -ml_cols % 256))).astype(BF16)
        w_cm = w_in_l[:, cfg.off_cm:].astype(BF16)
        H = cfg.mla_heads
        wq = jnp.pad(mla_w_uq[l].reshape(cfg.q_rank, H, cfg.nope + cfg.rope),
                     ((0, 0), (0, 0), (0, LANES - cfg.rope))).reshape(cfg.q_rank, H * 256).astype(BF16)
        wkv = mla_w_ukv[l].astype(BF16)
        b_full = jnp.repeat(cm_b_s[l].T, cfg.cm_dg, axis=1)
        keys = peer_sub_keys[l].astype(BF16)

        if l == 0:
            h1 = _norm_mod_call(cfg, xs, norm1_g[l], mods, 0, 1, cfg.T)
        else:
            xs, h1 = _res_norm_mod_call(cfg, xs, peer_raw, mods_all[l - 1], 5, norm1_g[l], mods, 0, 1, cfg.T)
        z_da = _mm_call(h1, w_da, BF16, "in_proj_da")
        z_ml = _mm_call(h1, w_ml, BF16, "in_proj_mla")
        z_cm = _mm_call(h1, w_cm, BF16, "in_proj_cm", rows=rows)
        q_da, k_da, v_da = _da_prep_call(cfg, z_da, cos_da, sin_da)
        q_ml, k_ml, v_ml = _mla_prep_call(cfg, z_ml, mla_q_norm_g[l].reshape(1, -1), mla_kv_norm_g[l].reshape(1, -1),
                                          wq, wkv, cos_ml, sin_ml)
        mshape = (rows, cfg.mix_w)
        da_args = (cfg, q_da, k_da, v_da, da_lambda[l], da_subln_g[l], lam_init)
        mix = _da_attn_call(*da_args, None, mshape, q_row0=0, sq=S, sk=cfg.Sk)
        mix = _mla_attn_call(cfg, q_ml, k_ml, v_ml, mix, mshape, q_row0=0, sq=S, sk=cfg.Sk)
        if not last:
            mix = _da_attn_call(*da_args, mix, mshape, q_row0=cfg.Tx, sq=C, sk=C)
            mix = _mla_attn_call(cfg, q_ml, k_ml, v_ml, mix, mshape, q_row0=cfg.Tx, sq=C, sk=C)
        mix = _chunk_mlp_call(cfg, z_cm, cm_v_norm_g[l], cm_w_s[l].astype(BF16), b_full, mix, mshape, rows)
        xs = _mm_res_call(cfg, mix, _cast_call(w_out, l), xs, mods, 2, rows, "out_proj")

        h2 = _norm_mod_call(cfg, xs, norm2_g[l], mods, 3, 4, rows)
        qp = _mm_call(h2, _cast_call(peer_w_q, l), BF16, "peer_query")
        er, cnt, ec, rk = _route_call(cfg, qp, keys, rows)
        gt = _peer_act_call(cfg, _cast_call(peer_u, l), h2, rows)
        peer_raw = _peer_out_call(cfg, gt, er, cnt, ec, rk, _cast_call(peer_v, l), rows)

    return _final_norm_call(cfg, xs, peer_raw, mods_all[L - 1], 5, final_norm_g).reshape(B, S, D)


def kernel(x, c, ctx, c_ctx, w_mod, b_mod, norm1_g, norm2_g, w_in, da_lambda, da_subln_g, mla_q_norm_g,
           mla_kv_norm_g, mla_w_uq, mla_w_ukv, cm_v_norm_g, cm_w_s, cm_b_s, w_out, peer_w_q, peer_sub_keys,
           peer_u, peer_v, final_norm_g):
    return _forward(_Cfg(), x, c, ctx, c_ctx, w_mod, b_mod, norm1_g, norm2_g, w_in, da_lambda, da_subln_g,
                    mla_q_norm_g, mla_kv_norm_g, mla_w_uq, mla_w_ukv, cm_v_norm_g, cm_w_s, cm_b_s, w_out,
                    peer_w_q, peer_sub_keys, peer_u, peer_v, final_norm_g)
```

```python
import functools
import math

import jax
import jax.numpy as jnp
from jax import lax
from jax.experimental import pallas as pl
from jax.experimental.pallas import tpu as pltpu

F32 = jnp.float32
BF16 = jnp.bfloat16
EPS = 1e-6
ROPE_BASE = 10000.0
LANES = 128
HEAD_W = 2 * LANES
MIB = 1024 * 1024
LOG2E = math.log2(math.e)
NT_DIMS = (((1,), (1,)), ((), ()))
TN_DIMS = (((0,), (0,)), ((), ()))


class _Cfg:
    def __init__(self, d_model=4096, batch=2, seq=8192, depth=2, grid_w=64, ctx_len=256,
                 da_heads=6, mla_heads=12, mla_q_rank=768, mla_kv_rank=512,
                 cm_groups=8, peer_heads=8):
        self.D, self.B, self.S, self.L = d_model, batch, seq, depth
        self.grid_w, self.C = grid_w, ctx_len
        self.da_heads, self.da_dk, self.da_dv = da_heads, 128, 256
        self.mla_heads, self.q_rank, self.kv_rank = mla_heads, mla_q_rank, mla_kv_rank
        self.nope, self.rope, self.mla_dv = 128, 64, 128
        self.cm_groups, self.cm_chunk, self.cm_dg = cm_groups, 128, 128
        self.peer_heads, self.peer_keys, self.peer_dkey, self.topk = peer_heads, 128, 256, 16
        self.da_w = da_heads * self.da_dv
        self.mla_w = mla_heads * self.mla_dv
        self.cm_w = cm_groups * self.cm_dg
        self.mix_w = self.da_w + self.mla_w + self.cm_w
        self.off_da_k = da_heads * 2 * self.da_dk
        self.off_da_v = 2 * self.off_da_k
        self.off_cq = self.off_da_v + self.da_w
        self.off_ckv = self.off_cq + mla_q_rank
        self.off_kr = self.off_ckv + mla_kv_rank
        self.off_cm = self.off_kr + self.rope
        self.in_cols = self.off_cm + 2 * self.cm_w
        self.n_exp = self.peer_keys * self.peer_keys
        self.Tx = batch * seq
        self.Tc = batch * ctx_len
        self.T = self.Tx + self.Tc
        self.Sk = ctx_len + seq


def _pick(n, cands):
    for c in cands:
        if n % c == 0:
            return c
    raise ValueError(f"no tile in {cands} divides {n}")


def _params(sem, vmem_mib):
    return pltpu.CompilerParams(dimension_semantics=sem, vmem_limit_bytes=int(vmem_mib * MIB))


def _rms(x, g):
    return x * lax.rsqrt(jnp.mean(x * x, axis=-1, keepdims=True) + EPS) * g


def _gelu(x):
    return 0.5 * x * (1.0 + jnp.tanh(math.sqrt(2.0 / math.pi) * (x + 0.044715 * (x * x * x))))


def _mod_kernel(c_ref, w_ref, b_ref, o_ref):
    c = c_ref[...]
    s = (c * jax.nn.sigmoid(c)).astype(BF16)
    o_ref[...] = jnp.dot(s, w_ref[...].astype(BF16), preferred_element_type=F32) + b_ref[...]


def _mod_call(cfg, c_all, w_mod, b_mod):
    L, D = cfg.L, cfg.D
    n = 6 * D
    tn = _pick(n, (512, 256, 128))
    return pl.pallas_call(
        _mod_kernel,
        out_shape=jax.ShapeDtypeStruct((L, 8, n), F32),
        grid=(L, n // tn),
        in_specs=[pl.BlockSpec((8, D), lambda l, j: (0, 0)),
                  pl.BlockSpec((None, D, tn), lambda l, j: (l, 0, j)),
                  pl.BlockSpec((None, 1, tn), lambda l, j: (l, 0, j))],
        out_specs=pl.BlockSpec((None, 8, tn), lambda l, j: (l, 0, j)),
        compiler_params=_params(("arbitrary", "arbitrary"), 2 * D * tn * 4 / MIB + 3 * D * tn * 2 / MIB + 4),
        name="adaln_mod",
    )(c_all, w_mod, b_mod.reshape(L, 1, n))


def _cast_kernel(w_ref, o_ref):
    o_ref[...] = w_ref[...].astype(o_ref.dtype)


def _cast_call(w, l):
    _, R, Cw = w.shape
    tr = _pick(R, (512, 256, 128))
    return pl.pallas_call(
        _cast_kernel,
        out_shape=jax.ShapeDtypeStruct((R, Cw), BF16),
        grid=(R // tr,),
        in_specs=[pl.BlockSpec((None, tr, Cw), lambda i: (l, i, 0))],
        out_specs=pl.BlockSpec((tr, Cw), lambda i: (i, 0)),
        compiler_params=_params(("arbitrary",), 2 * tr * Cw * 6 / MIB + 4),
        name="weight_cast",
    )(w)


def _norm_mod_kernel(x_ref, g_ref, sh_ref, sc_ref, o_ref):
    y = _rms(x_ref[...], g_ref[...])
    o_ref[...] = (y * (1.0 + sc_ref[...]) + sh_ref[...]).astype(o_ref.dtype)


def _res_norm_mod_kernel(x_ref, r_ref, gt_ref, g_ref, sh_ref, sc_ref, xo_ref, o_ref):
    x = x_ref[...] + gt_ref[...] * r_ref[...]
    xo_ref[...] = x
    o_ref[...] = (_rms(x, g_ref[...]) * (1.0 + sc_ref[...]) + sh_ref[...]).astype(o_ref.dtype)


def _seg_of_tile(cfg, tm):
    return lambda i: jnp.minimum((i * tm) // cfg.S, cfg.B)


def _res_norm_mod_call(cfg, x, raw, gate_mods, k_gate, g, mods, k_shift, k_scale, rows):
    D = cfg.D
    tm = _pick(math.gcd(cfg.S, cfg.Tc), (256, 128))
    seg = _seg_of_tile(cfg, tm)
    row = pl.BlockSpec((tm, D), lambda i: (i, 0))
    vec = lambda k: pl.BlockSpec((None, None, 1, D), lambda i: (seg(i), k, 0, 0))
    return pl.pallas_call(
        _res_norm_mod_kernel,
        out_shape=(jax.ShapeDtypeStruct((rows, D), F32), jax.ShapeDtypeStruct((rows, D), BF16)),
        grid=(rows // tm,),
        in_specs=[row, row, vec(k_gate), pl.BlockSpec((1, D), lambda i: (0, 0)), vec(k_shift), vec(k_scale)],
        out_specs=(row, row),
        compiler_params=_params(("arbitrary",), 12 * tm * D * 4 / MIB + 4),
        name="residual_norm_modulate",
    )(x, raw, gate_mods, g.reshape(1, D), mods, mods)


def _norm_mod_call(cfg, x, g, mods, k_shift, k_scale, rows):
    D = cfg.D
    tm = _pick(math.gcd(cfg.S, cfg.Tc), (512, 256, 128))
    seg = _seg_of_tile(cfg, tm)
    return pl.pallas_call(
        _norm_mod_kernel,
        out_shape=jax.ShapeDtypeStruct((rows, D), BF16),
        grid=(rows // tm,),
        in_specs=[pl.BlockSpec((tm, D), lambda i: (i, 0)),
                  pl.BlockSpec((1, D), lambda i: (0, 0)),
                  pl.BlockSpec((None, None, 1, D), lambda i: (seg(i), k_shift, 0, 0)),
                  pl.BlockSpec((None, None, 1, D), lambda i: (seg(i), k_scale, 0, 0))],
        out_specs=pl.BlockSpec((tm, D), lambda i: (i, 0)),
        compiler_params=_params(("arbitrary",), 5 * tm * D * 4 / MIB + 4),
        name="norm_modulate",
    )(x, g.reshape(1, D), mods, mods)


def _mm_kernel(a_ref, b_ref, o_ref):
    o_ref[...] = jnp.dot(a_ref[...], b_ref[...], preferred_element_type=F32).astype(o_ref.dtype)


MM_VMEM_BUDGET_MIB = 48


def _mm_vmem_mib(tm, tn, K, osz):
    return (2 * (tm * K * 2 + K * tn * 2 + tm * tn * osz) + tm * tn * 4) / MIB + 4


def _mm_tiles(M, K, N, osz):
    fits = [(tm * tn / (tm + tn), tn, tm)
            for tm in (1536, 1024, 512, 256, 128) for tn in (N, 1536, 1024, 768, 512, 256, 128)
            if M % tm == 0 and N % tn == 0 and (tn == N or tn % LANES == 0)
            and _mm_vmem_mib(tm, tn, K, osz) <= MM_VMEM_BUDGET_MIB]
    if not fits:
        raise ValueError(f"no matmul tiling for {(M, K, N)}")
    _, tn, tm = max(fits)
    return tm, tn


def _mm_call(a, b, out_dtype, name, rows=None):
    M = a.shape[0] if rows is None else rows
    K, N = b.shape
    osz = jnp.dtype(out_dtype).itemsize
    tm, tn = _mm_tiles(M, K, N, osz)
    vm = _mm_vmem_mib(tm, tn, K, osz)
    return pl.pallas_call(
        _mm_kernel,
        out_shape=jax.ShapeDtypeStruct((M, N), out_dtype),
        grid=(M // tm, N // tn),
        in_specs=[pl.BlockSpec((tm, K), lambda i, j: (i, 0)),
                  pl.BlockSpec((K, tn), lambda i, j: (0, j))],
        out_specs=pl.BlockSpec((tm, tn), lambda i, j: (i, j)),
        compiler_params=_params(("arbitrary", "arbitrary"), vm),
        name=name,
    )(a, b)


def _mm_res_kernel(a_ref, b_ref, x_ref, g_ref, o_ref):
    acc = jnp.dot(a_ref[...], b_ref[...], preferred_element_type=F32)
    o_ref[...] = x_ref[...] + g_ref[...] * acc


def _mm_res_call(cfg, a, b, x, mods, k_gate, rows, name):
    K, N = b.shape
    tm = _pick(math.gcd(cfg.S, cfg.Tc), (512, 256, 128))
    tn = _pick(N, (1024, 512, 256, 128))
    seg = _seg_of_tile(cfg, tm)
    vm = 2 * (tm * K * 2 + K * tn * 2 + 2 * tm * tn * 4) / MIB + tm * tn * 4 / MIB + 4
    return pl.pallas_call(
        _mm_res_kernel,
        out_shape=jax.ShapeDtypeStruct((rows, N), F32),
        grid=(rows // tm, N // tn),
        in_specs=[pl.BlockSpec((tm, K), lambda i, j: (i, 0)),
                  pl.BlockSpec((K, tn), lambda i, j: (0, j)),
                  pl.BlockSpec((tm, tn), lambda i, j: (i, j)),
                  pl.BlockSpec((None, None, 1, tn), lambda i, j: (seg(i), k_gate, 0, j))],
        out_specs=pl.BlockSpec((tm, tn), lambda i, j: (i, j)),
        compiler_params=_params(("arbitrary", "arbitrary"), vm),
        name=name,
    )(a, b, x, mods)


def _key_block(cfg, tm):
    n_lat = cfg.Tx // tm
    per_b = cfg.S // tm
    per_c = cfg.C // tm

    def batch(i):
        return jnp.where(i < n_lat, i // per_b, (i - n_lat) // per_c)

    def row(i):
        return jnp.where(i < n_lat, per_c + i % per_b, (i - n_lat) % per_c)

    return batch, row


def _swap_halves(x, lane, half):
    first = (lane % (2 * half)) < half
    return jnp.where(first, pltpu.roll(x, LANES - half, 1), pltpu.roll(x, half, 1))


def _da_prep_kernel(q_ref, k_ref, v_ref, cos_ref, sin_ref, qo_ref, ko_ref, vo_ref, *, groups, qscale):
    cos = cos_ref[...]
    sin = sin_ref[...]
    lane = lax.broadcasted_iota(jnp.int32, cos.shape, 1)

    def rope(x):
        return x * cos + _swap_halves(x, lane, 32) * sin

    for g in range(groups):
        sl = slice(g * LANES, (g + 1) * LANES)
        qo_ref[:, sl] = (rope(q_ref[:, sl].astype(F32)) * qscale).astype(BF16)
        ko_ref[:, sl] = rope(k_ref[:, sl].astype(F32)).astype(BF16)
    vo_ref[...] = v_ref[...]


def _da_prep_call(cfg, z_da, cos, sin):
    W = cfg.da_w
    tm = min(cfg.C, HEAD_W)
    kb, kr = _key_block(cfg, tm)
    tok = lambda c: pl.BlockSpec((tm, W), lambda i: (i, c))
    tab = pl.BlockSpec((tm, LANES), lambda i: (i, 0))
    key = pl.BlockSpec((None, tm, W), lambda i: (kb(i), kr(i), 0))
    return pl.pallas_call(
        functools.partial(_da_prep_kernel, groups=W // LANES, qscale=cfg.da_dk ** -0.5 * LOG2E),
        out_shape=(jax.ShapeDtypeStruct((cfg.T, W), BF16),
                   jax.ShapeDtypeStruct((cfg.B, cfg.Sk, W), BF16),
                   jax.ShapeDtypeStruct((cfg.B, cfg.Sk, W), BF16)),
        grid=(cfg.T // tm,),
        in_specs=[tok(0), tok(1), tok(2), tab, tab],
        out_specs=(tok(0), key, key),
        compiler_params=_params(("arbitrary",), 12 * tm * W * 2 / MIB + 8),
        name="da_prep",
    )(z_da, z_da, z_da, cos, sin)


def _mla_prep_kernel(z_ref, gq_ref, gkv_ref, wq_ref, wkv_ref, cos_ref, sin_ref,
                     qo_ref, ko_ref, vo_ref, *, heads, q_rank, kv_rank, qscale):
    z = z_ref[...].astype(F32)
    cos = cos_ref[...]
    sin = sin_ref[...]
    lane = lax.broadcasted_iota(jnp.int32, cos.shape, 1)

    def rope(x):
        return x * cos + _swap_halves(x, lane, 16) * sin

    cq = _rms(z[:, :q_rank], gq_ref[...]).astype(BF16)
    ckv = _rms(z[:, q_rank:q_rank + kv_rank], gkv_ref[...]).astype(BF16)
    kpe = rope(z[:, q_rank + kv_rank:q_rank + kv_rank + LANES]).astype(BF16)
    q = jnp.dot(cq, wq_ref[...], preferred_element_type=F32)
    kv = jnp.dot(ckv, wkv_ref[...], preferred_element_type=F32)
    for h in range(heads):
        a, b, c = 2 * h * LANES, (2 * h + 1) * LANES, (2 * h + 2) * LANES
        qo_ref[:, a:b] = (q[:, a:b] * qscale).astype(BF16)
        qo_ref[:, b:c] = (rope(q[:, b:c]) * qscale).astype(BF16)
        ko_ref[:, a:b] = kv[:, a:b].astype(BF16)
        ko_ref[:, b:c] = kpe
        vo_ref[:, a:b] = kv[:, b:c].astype(BF16)
        vo_ref[:, b:c] = jnp.ones((z.shape[0], LANES), BF16)


def _mla_prep_call(cfg, z_mla, gq, gkv, wq, wkv, cos, sin):
    H = cfg.mla_heads
    Wz = z_mla.shape[1]
    tm = min(cfg.C, HEAD_W)
    kb, kr = _key_block(cfg, tm)
    full = lambda a: pl.BlockSpec(a.shape, lambda i: (0,) * a.ndim)
    tab = pl.BlockSpec((tm, LANES), lambda i: (i, 0))
    return pl.pallas_call(
        functools.partial(_mla_prep_kernel, heads=H, q_rank=cfg.q_rank, kv_rank=cfg.kv_rank,
                          qscale=(cfg.nope + cfg.rope) ** -0.5 * LOG2E),
        out_shape=(jax.ShapeDtypeStruct((cfg.T, H * HEAD_W), BF16),
                   jax.ShapeDtypeStruct((cfg.B, cfg.Sk, H * HEAD_W), BF16),
                   jax.ShapeDtypeStruct((cfg.B, cfg.Sk, H * HEAD_W), BF16)),
        grid=(cfg.T // tm,),
        in_specs=[pl.BlockSpec((tm, Wz), lambda i: (i, 0)), full(gq), full(gkv), full(wq), full(wkv), tab, tab],
        out_specs=(pl.BlockSpec((tm, H * HEAD_W), lambda i: (i, 0)),
                   pl.BlockSpec((None, tm, H * HEAD_W), lambda i: (kb(i), kr(i), 0)),
                   pl.BlockSpec((None, tm, H * HEAD_W), lambda i: (kb(i), kr(i), 0))),
        compiler_params=_params(("arbitrary",), 2 * (wq.size + wkv.size) * 2 / MIB + 24 * tm * H * HEAD_W / MIB + 8),
        name="mla_prep",
    )(z_mla, gq, gkv, wq, wkv, cos, sin)


ATTN_SUB_ROWS = 512


def _online_step(q, k, v, m, l, acc):
    s = lax.dot_general(q, k, NT_DIMS, preferred_element_type=F32)
    m_new = jnp.maximum(m, jnp.max(s, axis=-1, keepdims=True))
    alpha = jnp.exp2(m - m_new)
    p = jnp.exp2(s - m_new)
    l = alpha * l + jnp.sum(p, axis=-1, keepdims=True)
    acc = alpha * acc + jnp.dot(p.astype(BF16), v, preferred_element_type=F32)
    return m_new, l, acc


def _da_attn_kernel(lam_ref, g_ref, q_ref, k_ref, v_ref, o_ref, *, tk, nk, sub, lam_init):
    lv = lam_ref[...]
    lam = (jnp.exp(jnp.sum(lv[0:1] * lv[1:2], axis=-1, keepdims=True))
           - jnp.exp(jnp.sum(lv[2:3] * lv[3:4], axis=-1, keepdims=True)) + lam_init)
    neg = jnp.full((sub, 1), -jnp.inf, F32)
    zero = jnp.zeros((sub, 1), F32)
    acc0 = jnp.zeros((sub, v_ref.shape[1]), F32)
    for q0 in range(0, q_ref.shape[0], sub):
        q1 = q_ref[q0:q0 + sub, :LANES]
        q2 = q_ref[q0:q0 + sub, LANES:]
        m1, l1, a1, m2, l2, a2 = neg, zero, acc0, neg, zero, acc0
        for kk in range(nk):
            rows = slice(kk * tk, (kk + 1) * tk)
            v = v_ref[rows, :]
            m1, l1, a1 = _online_step(q1, k_ref[rows, :LANES], v, m1, l1, a1)
            m2, l2, a2 = _online_step(q2, k_ref[rows, LANES:], v, m2, l2, a2)
        o = a1 / l1 - lam * (a2 / l2)
        o_ref[q0:q0 + sub, :] = (_rms(o, g_ref[...]) * (1.0 - lam_init)).astype(o_ref.dtype)


def _into_mix(kernel, n_in, mix, mix_shape):
    out_shape = jax.ShapeDtypeStruct(mix_shape, BF16)
    if mix is None:
        return kernel, [], (), {}, out_shape

    def with_unused_mix_ref(*refs):
        return kernel(*refs[:n_in], *refs[n_in + 1:])

    return with_unused_mix_ref, [pl.BlockSpec(memory_space=pl.ANY)], (mix,), {n_in: 0}, out_shape


def _da_attn_call(cfg, q, k, v, lam_vec, subln_g, lam_init, mix, mix_shape, *, q_row0, sq, sk):
    H, dv = cfg.da_heads, cfg.da_dv
    tq = _pick(sq, (512, 256, 128))
    sub = min(tq, ATTN_SUB_ROWS)
    tk = _pick(sk, (768, 512, 256, 128))
    nq = sq // tq
    q0 = q_row0 // tq
    vm = 2 * (2 * sk * HEAD_W * 2) / MIB + 16 * sub * tk * 4 / MIB + 12
    kern, mix_spec, mix_arg, alias, out_shape = _into_mix(
        functools.partial(_da_attn_kernel, tk=tk, nk=sk // tk, sub=sub, lam_init=lam_init), 5, mix, mix_shape)
    return pl.pallas_call(
        kern,
        out_shape=out_shape,
        input_output_aliases=alias,
        grid=(cfg.B, H, nq),
        in_specs=[pl.BlockSpec((4, LANES), lambda b, h, i: (0, 0)),
                  pl.BlockSpec((1, dv), lambda b, h, i: (0, 0)),
                  pl.BlockSpec((tq, HEAD_W), lambda b, h, i: (q0 + b * nq + i, h)),
                  pl.BlockSpec((None, sk, HEAD_W), lambda b, h, i: (b, 0, h)),
                  pl.BlockSpec((None, sk, dv), lambda b, h, i: (b, 0, h))] + mix_spec,
        out_specs=pl.BlockSpec((tq, dv), lambda b, h, i: (q0 + b * nq + i, h)),
        compiler_params=_params(("arbitrary", "arbitrary", "arbitrary"), vm),
        name="diff_attention",
    )(lam_vec, subln_g.reshape(1, dv), q, k, v, *mix_arg)


def _mla_attn_kernel(q_ref, k_ref, v_ref, o_ref, *, tk, nk, sub):
    dv = o_ref.shape[1]
    for q0 in range(0, q_ref.shape[0], sub):
        q = q_ref[q0:q0 + sub, :]
        m, acc = jnp.full((sub, 1), -jnp.inf, F32), jnp.zeros((sub, v_ref.shape[1]), F32)
        for kk in range(nk):
            rows = slice(kk * tk, (kk + 1) * tk)
            s = lax.dot_general(q, k_ref[rows, :], NT_DIMS, preferred_element_type=F32)
            m_new = jnp.maximum(m, jnp.max(s, axis=-1, keepdims=True))
            p = jnp.exp2(s - m_new).astype(BF16)
            acc = jnp.exp2(m - m_new) * acc + jnp.dot(p, v_ref[rows, :], preferred_element_type=F32)
            m = m_new
        o_ref[q0:q0 + sub, :] = (acc[:, :dv] / acc[:, dv:dv + 1]).astype(o_ref.dtype)


def _mla_attn_call(cfg, q, k, v, mix, mix_shape, *, q_row0, sq, sk):
    H, dv = cfg.mla_heads, cfg.mla_dv
    col0 = cfg.da_w // dv
    tq = _pick(sq, (1024, 512, 256, 128))
    sub = min(tq, ATTN_SUB_ROWS)
    tk = _pick(sk, (768, 512, 256, 128))
    nq = sq // tq
    q0 = q_row0 // tq
    vm = 2 * (sk * 2 * HEAD_W * 2) / MIB + 6 * sub * tk * 4 / MIB + 16
    kern, mix_spec, mix_arg, alias, out_shape = _into_mix(
        functools.partial(_mla_attn_kernel, tk=tk, nk=sk // tk, sub=sub), 3, mix, mix_shape)
    return pl.pallas_call(
        kern,
        out_shape=out_shape,
        input_output_aliases=alias,
        grid=(cfg.B, H, nq),
        in_specs=[pl.BlockSpec((tq, HEAD_W), lambda b, h, i: (q0 + b * nq + i, h)),
                  pl.BlockSpec((None, sk, HEAD_W), lambda b, h, i: (b, 0, h)),
                  pl.BlockSpec((None, sk, 2 * dv), lambda b, h, i: (b, 0, h))] + mix_spec,
        out_specs=pl.BlockSpec((tq, dv), lambda b, h, i: (q0 + b * nq + i, col0 + h)),
        compiler_params=_params(("arbitrary", "arbitrary", "arbitrary"), vm),
        name="mla_attention",
    )(q, k, v, *mix_arg)


def _chunk_mlp_kernel(z_ref, g_ref, ws_ref, b_ref, o_ref, *, groups, chunk):
    uv = _gelu(z_ref[...].astype(F32))
    w = groups * LANES
    u = uv[:, :w]
    v = _rms(uv[:, w:], g_ref[...]).astype(BF16)
    for c in range(z_ref.shape[0] // chunk):
        rows = slice(c * chunk, (c + 1) * chunk)
        for g in range(groups):
            cols = slice(g * LANES, (g + 1) * LANES)
            y = jnp.dot(ws_ref[g], v[rows, cols], preferred_element_type=F32) + b_ref[:, cols]
            o_ref[rows, cols] = (u[rows, cols] * y).astype(o_ref.dtype)


def _chunk_mlp_call(cfg, z_cm, g, ws, b_full, mix, mix_shape, rows):
    W = cfg.cm_w
    col0 = (cfg.da_w + cfg.mla_w) // W
    tm = _pick(math.gcd(cfg.S, cfg.Tc), (256, 128))
    kern, mix_spec, mix_arg, alias, out_shape = _into_mix(
        functools.partial(_chunk_mlp_kernel, groups=cfg.cm_groups, chunk=cfg.cm_chunk), 4, mix, mix_shape)
    return pl.pallas_call(
        kern,
        out_shape=out_shape,
        input_output_aliases=alias,
        grid=(rows // tm,),
        in_specs=[pl.BlockSpec((tm, 2 * W), lambda i: (i, 0)),
                  pl.BlockSpec((1, W), lambda i: (0, 0)),
                  pl.BlockSpec(ws.shape, lambda i: (0, 0, 0)),
                  pl.BlockSpec(b_full.shape, lambda i: (0, 0))] + mix_spec,
        out_specs=pl.BlockSpec((tm, W), lambda i: (i, col0)),
        compiler_params=_params(("arbitrary",), 40 * tm * W / MIB + 8),
        name="chunk_mlp",
    )(z_cm, g.reshape(1, W), ws, b_full, *mix_arg)


INT32_MIN = -2 ** 31


def _order_key(x):
    b = lax.bitcast_convert_type(x, jnp.int32)
    return jnp.where(b < 0, b ^ 0x7FFFFFFF, b)


def _order_key_inv(k):
    return lax.bitcast_convert_type(jnp.where(k < 0, k ^ 0x7FFFFFFF, k), F32)


def _top_k_sublanes(keys, pos, topk):
    iota_k = lax.broadcasted_iota(jnp.int32, (topk, keys[0].shape[1]), 0)

    def body(k, carry):
        out = []
        for (work, vals), p in zip(carry, pos):
            m = jnp.max(work, axis=0, keepdims=True)
            idx = jnp.min(jnp.where(work == m, p, jnp.inf), axis=0, keepdims=True)
            out.append((jnp.where(p == idx, INT32_MIN + k, work), jnp.where(iota_k == k, m, vals)))
        return tuple(out)

    init = tuple((w, jnp.zeros((topk, w.shape[1]), jnp.int32)) for w in keys)
    res = lax.fori_loop(0, topk, body, init)
    return [(jnp.where(w < INT32_MIN + topk, w - INT32_MIN, topk), v) for w, v in res]


def _pair_blocks(topk):
    return [(0, topk)] + [(a, 8) for a in range(1, 8)]


def _route_kernel(q_ref, keys_ref, er_ref, cnt_ref, ec_ref, rk_ref, *, topk):
    half = q_ref.shape[1] // 2
    tm = q_ref.shape[0]
    s_row = lax.dot_general(keys_ref[0], q_ref[:, :half], NT_DIMS, preferred_element_type=F32)
    s_col = lax.dot_general(keys_ref[1], q_ref[:, half:], NT_DIMS, preferred_element_type=F32)
    iota = lax.broadcasted_iota(jnp.int32, (s_row.shape[0], LANES), 0).astype(F32)
    groups = [slice(g * LANES, (g + 1) * LANES) for g in range(tm // LANES)]
    res = _top_k_sublanes([_order_key(s[:, lanes]) for lanes in groups for s in (s_row, s_col)],
                          [iota] * (2 * len(groups)), topk)
    rank_r = jnp.concatenate([r for r, _ in res[0::2]], axis=1)
    rank_c = jnp.concatenate([r for r, _ in res[1::2]], axis=1)
    top_r = jnp.concatenate([_order_key_inv(v) for _, v in res[0::2]], axis=1)
    top_c = jnp.concatenate([_order_key_inv(v) for _, v in res[1::2]], axis=1)

    blocks = _pair_blocks(topk)
    cand_parts, pos_parts, valid_parts = [], [], []
    for a, nb in blocks:
        b_iota = lax.broadcasted_iota(jnp.int32, (nb, tm), 0)
        cand_parts.append(top_r[a:a + 1, :] + top_c[:nb, :])
        pos_parts.append((a * topk + b_iota).astype(F32))
        valid_parts.append((a + 1) * (b_iota + 1) <= topk)
    a_iota = lax.broadcasted_iota(jnp.int32, (topk - 8, tm), 0) + 8
    cand_parts.append(top_r[8:, :] + top_c[0:1, :])
    pos_parts.append((a_iota * topk).astype(F32))
    valid_parts.append(a_iota < topk)
    cand = jnp.concatenate(cand_parts, axis=0)
    pos = jnp.concatenate(pos_parts, axis=0)
    valid = jnp.concatenate(valid_parts, axis=0)
    cand_key = jnp.where(valid, _order_key(cand), INT32_MIN + topk)
    (rank_p, best), = _top_k_sublanes([cand_key], [pos], topk)
    chosen = rank_p < topk
    best0 = _order_key_inv(best[0:1, :])
    z = jnp.sum(jnp.where(chosen, jnp.exp(cand - best0), 0.0), axis=0, keepdims=True)
    chosen_f = chosen.astype(F32)
    cnt = jnp.zeros(s_row.shape, F32)
    row0 = 0
    for a, nb in blocks:
        cnt_a = jnp.sum(chosen_f[row0:row0 + nb, :], axis=0, keepdims=True)
        cnt = jnp.where(rank_r == a, cnt_a, cnt)
        row0 += nb
    for a in range(8, topk):
        cnt = jnp.where(rank_r == a, chosen_f[row0 + a - 8:row0 + a - 7, :], cnt)
    er_ref[...] = jnp.where(rank_r < topk, jnp.exp(s_row - top_r[0:1, :]), 0.0)
    cnt_ref[...] = cnt
    ec_ref[...] = jnp.where(rank_c < topk, jnp.exp(s_col - top_c[0:1, :]) / z, 0.0)
    rk_ref[...] = rank_c.astype(F32)


def _route_call(cfg, qp, keys, rows):
    H, nk = cfg.peer_heads, cfg.peer_keys
    tm = _pick(rows, (256, 128))
    out = lambda dt: jax.ShapeDtypeStruct((H, nk, rows), dt)
    ospec = pl.BlockSpec((None, nk, tm), lambda i, h: (h, 0, i))
    return pl.pallas_call(
        functools.partial(_route_kernel, topk=cfg.topk),
        out_shape=(out(F32), out(F32), out(F32), out(F32)),
        grid=(rows // tm, H),
        in_specs=[pl.BlockSpec((tm, cfg.peer_dkey), lambda i, h: (i, h)),
                  pl.BlockSpec((None, 2, nk, cfg.peer_dkey // 2), lambda i, h: (h, 0, 0, 0))],
        out_specs=(ospec, ospec, ospec, ospec),
        compiler_params=_params(("arbitrary", "arbitrary"), 32),
        name="peer_route",
    )(qp, keys)


def _peer_act_kernel(u_ref, h_ref, o_ref):
    o_ref[...] = _gelu(lax.dot_general(u_ref[...], h_ref[...], NT_DIMS, preferred_element_type=F32)).astype(o_ref.dtype)


def _peer_act_call(cfg, u, h2, rows):
    D = cfg.D
    tn, tm = _mm_tiles(cfg.n_exp, D, rows, 2)
    return pl.pallas_call(
        _peer_act_kernel,
        out_shape=jax.ShapeDtypeStruct((cfg.n_exp, rows), BF16),
        grid=(rows // tm, cfg.n_exp // tn),
        in_specs=[pl.BlockSpec((tn, D), lambda i, n: (n, 0)),
                  pl.BlockSpec((tm, D), lambda i, n: (i, 0))],
        out_specs=pl.BlockSpec((tn, tm), lambda i, n: (n, i)),
        compiler_params=_params(("arbitrary", "arbitrary"), _mm_vmem_mib(tn, tm, D, 2) + 4 * tn * tm * 4 / MIB),
        name="peer_activations",
    )(u, h2)


def _peer_out_kernel(gt_ref, er_ref, cnt_ref, ec_ref, rk_ref, v_ref, o_ref, *, heads):
    @pl.when(pl.program_id(1) == 0)
    def _():
        o_ref[...] = jnp.zeros_like(o_ref)

    tk, tm = gt_ref.shape
    R = tk // LANES
    w = jnp.zeros((R, LANES, tm), F32)
    for h in range(heads):
        hit = rk_ref[h][None, :, :] < cnt_ref[h][:, None, :]
        w = w + jnp.where(hit, ec_ref[h][None, :, :], 0.0) * er_ref[h][:, None, :]
    pt = (w.reshape(tk, tm) * gt_ref[...].astype(F32)).astype(BF16)
    o_ref[...] += lax.dot_general(pt, v_ref[...], TN_DIMS, preferred_element_type=F32)


def _peer_out_call(cfg, gt, er, cnt, ec, rk, v, rows):
    H, nk, D = cfg.peer_heads, cfg.peer_keys, cfg.D
    tm = _pick(rows, (512, 256, 128))
    tk = 1024
    R = tk // LANES
    row_spec = pl.BlockSpec((H, R, tm), lambda i, k: (0, k, i))
    col_spec = pl.BlockSpec((H, nk, tm), lambda i, k: (0, 0, i))
    vm = 2 * (tk * tm * 2 + tk * D * 2 + tm * D * 4 + 2 * H * nk * tm * 4) / MIB + 6 * tk * tm * 4 / MIB + 4
    return pl.pallas_call(
        functools.partial(_peer_out_kernel, heads=H),
        out_shape=jax.ShapeDtypeStruct((rows, D), F32),
        grid=(rows // tm, cfg.n_exp // tk),
        in_specs=[pl.BlockSpec((tk, tm), lambda i, k: (k, i)), row_spec, row_spec, col_spec, col_spec,
                  pl.BlockSpec((tk, D), lambda i, k: (k, 0))],
        out_specs=pl.BlockSpec((tm, D), lambda i, k: (i, 0)),
        compiler_params=_params(("arbitrary", "arbitrary"), vm),
        name="peer_output",
    )(gt, er, cnt, ec, rk, v)


def _final_norm_kernel(x_ref, r_ref, gt_ref, g_ref, o_ref):
    o_ref[...] = _rms(x_ref[...] + gt_ref[...] * r_ref[...], g_ref[...])


def _final_norm_call(cfg, x, raw, gate_mods, k_gate, g):
    D = cfg.D
    tm = _pick(math.gcd(cfg.S, cfg.Tc), (256, 128))
    seg = _seg_of_tile(cfg, tm)
    return pl.pallas_call(
        _final_norm_kernel,
        out_shape=jax.ShapeDtypeStruct((cfg.Tx, D), F32),
        grid=(cfg.Tx // tm,),
        in_specs=[pl.BlockSpec((tm, D), lambda i: (i, 0)), pl.BlockSpec((tm, D), lambda i: (i, 0)),
                  pl.BlockSpec((None, None, 1, D), lambda i: (seg(i), k_gate, 0, 0)),
                  pl.BlockSpec((1, D), lambda i: (0, 0))],
        out_specs=pl.BlockSpec((tm, D), lambda i: (i, 0)),
        compiler_params=_params(("arbitrary",), 8 * tm * D * 4 / MIB + 4),
        name="final_norm",
    )(x, raw, gate_mods, g.reshape(1, D))


def _rope_tables(cfg, half, pad):
    pos = jnp.arange(cfg.S, dtype=jnp.int32)
    inv = ROPE_BASE ** (-jnp.arange(half, dtype=F32) / half)
    ang_r = (pos // cfg.grid_w).astype(F32)[:, None] * inv[None, :]
    ang_c = (pos % cfg.grid_w).astype(F32)[:, None] * inv[None, :]
    cos = jnp.concatenate([jnp.cos(ang_r)] * 2 + [jnp.cos(ang_c)] * 2, axis=-1)
    sin = jnp.concatenate([-jnp.sin(ang_r), jnp.sin(ang_r), -jnp.sin(ang_c), jnp.sin(ang_c)], axis=-1)
    cos = jnp.concatenate([jnp.tile(cos, (cfg.B, 1)), jnp.ones((cfg.Tc, 4 * half), F32)], axis=0)
    sin = jnp.concatenate([jnp.tile(sin, (cfg.B, 1)), jnp.zeros((cfg.Tc, 4 * half), F32)], axis=0)
    if pad:
        cos = jnp.pad(cos, ((0, 0), (0, pad)))
        sin = jnp.pad(sin, ((0, 0), (0, pad)))
    return cos, sin


def _forward(cfg, x, c, ctx, c_ctx, w_mod, b_mod, norm1_g, norm2_g, w_in, da_lambda, da_subln_g,
             mla_q_norm_g, mla_kv_norm_g, mla_w_uq, mla_w_ukv, cm_v_norm_g, cm_w_s, cm_b_s,
             w_out, peer_w_q, peer_sub_keys, peer_u, peer_v, final_norm_g):
    B, S, C, D, L = cfg.B, cfg.S, cfg.C, cfg.D, cfg.L
    assert cfg.da_dk == LANES and cfg.nope == LANES and cfg.peer_keys == LANES and cfg.topk == 16
    assert C % LANES == 0 and S % C == 0 and cfg.q_rank % LANES == 0 and cfg.kv_rank % LANES == 0
    assert (cfg.da_w + cfg.mla_w) % cfg.cm_w == 0

    xs = jnp.concatenate([x.reshape(B * S, D), ctx.reshape(B * C, D)], axis=0)
    c_all = jnp.concatenate([c, c_ctx[None, :], jnp.zeros((8 - B - 1, D), F32)], axis=0)
    mods_all = _mod_call(cfg, c_all, w_mod, b_mod).reshape(L, 8, 6, 1, D)
    cos_da, sin_da = _rope_tables(cfg, 32, 0)
    cos_ml, sin_ml = _rope_tables(cfg, 16, LANES - cfg.rope)

    for l in range(L):
        last = l == L - 1
        rows = cfg.Tx if last else cfg.T
        mods = mods_all[l]
        lam_init = 0.8 - 0.6 * math.exp(-0.3 * l)

        w_in_l = w_in[l]
        w_da = w_in_l[:, :cfg.off_cq].astype(BF16)
        ml_cols = cfg.off_cm - cfg.off_cq
        w_ml = jnp.pad(w_in_l[:, cfg.off_cq:cfg.off_cm], ((0, 0), (0, -ml_cols % HEAD_W))).astype(BF16)
        w_cm = w_in_l[:, cfg.off_cm:].astype(BF16)
        H = cfg.mla_heads
        wq = jnp.pad(mla_w_uq[l].reshape(cfg.q_rank, H, cfg.nope + cfg.rope),
                     ((0, 0), (0, 0), (0, LANES - cfg.rope))).reshape(cfg.q_rank, H * HEAD_W).astype(BF16)
        wkv = mla_w_ukv[l].astype(BF16)
        b_full = jnp.repeat(cm_b_s[l].T, cfg.cm_dg, axis=1)
        keys = peer_sub_keys[l].astype(BF16)

        if l == 0:
            h1 = _norm_mod_call(cfg, xs, norm1_g[l], mods, 0, 1, cfg.T)
        else:
            xs, h1 = _res_norm_mod_call(cfg, xs, peer_raw, mods_all[l - 1], 5, norm1_g[l], mods, 0, 1, cfg.T)
        z_da = _mm_call(h1, w_da, BF16, "in_proj_da")
        z_ml = _mm_call(h1, w_ml, BF16, "in_proj_mla")
        z_cm = _mm_call(h1, w_cm, BF16, "in_proj_cm", rows=rows)
        q_da, k_da, v_da = _da_prep_call(cfg, z_da, cos_da, sin_da)
        q_ml, k_ml, v_ml = _mla_prep_call(cfg, z_ml, mla_q_norm_g[l].reshape(1, -1), mla_kv_norm_g[l].reshape(1, -1),
                                          wq, wkv, cos_ml, sin_ml)
        mshape = (rows, cfg.mix_w)
        da_args = (cfg, q_da, k_da, v_da, da_lambda[l], da_subln_g[l], lam_init)
        mix = _da_attn_call(*da_args, None, mshape, q_row0=0, sq=S, sk=cfg.Sk)
        mix = _mla_attn_call(cfg, q_ml, k_ml, v_ml, mix, mshape, q_row0=0, sq=S, sk=cfg.Sk)
        if not last:
            mix = _da_attn_call(*da_args, mix, mshape, q_row0=cfg.Tx, sq=C, sk=C)
            mix = _mla_attn_call(cfg, q_ml, k_ml, v_ml, mix, mshape, q_row0=cfg.Tx, sq=C, sk=C)
        mix = _chunk_mlp_call(cfg, z_cm, cm_v_norm_g[l], cm_w_s[l].astype(BF16), b_full, mix, mshape, rows)
        xs = _mm_res_call(cfg, mix, _cast_call(w_out, l), xs, mods, 2, rows, "out_proj")

        h2 = _norm_mod_call(cfg, xs, norm2_g[l], mods, 3, 4, rows)
        qp = _mm_call(h2, _cast_call(peer_w_q, l), BF16, "peer_query")
        er, cnt, ec, rk = _route_call(cfg, qp, keys, rows)
        gt = _peer_act_call(cfg, _cast_call(peer_u, l), h2, rows)
        peer_raw = _peer_out_call(cfg, gt, er, cnt, ec, rk, _cast_call(peer_v, l), rows)

    return _final_norm_call(cfg, xs, peer_raw, mods_all[L - 1], 5, final_norm_g).reshape(B, S, D)


def kernel(x, c, ctx, c_ctx, w_mod, b_mod, norm1_g, norm2_g, w_in, da_lambda, da_subln_g, mla_q_norm_g,
           mla_kv_norm_g, mla_w_uq, mla_w_ukv, cm_v_norm_g, cm_w_s, cm_b_s, w_out, peer_w_q, peer_sub_keys,
           peer_u, peer_v, final_norm_g):
    return _forward(_Cfg(), x, c, ctx, c_ctx, w_mod, b_mod, norm1_g, norm2_g, w_in, da_lambda, da_subln_g,
                    mla_q_norm_g, mla_kv_norm_g, mla_w_uq, mla_w_ukv, cm_v_norm_g, cm_w_s, cm_b_s, w_out,
                    peer_w_q, peer_sub_keys, peer_u, peer_v, final_norm_g)
```

```python
import functools
import math

import jax
import jax.numpy as jnp
from jax import lax
from jax.experimental import pallas as pl
from jax.experimental.pallas import tpu as pltpu

F32 = jnp.float32
BF16 = jnp.bfloat16
EPS = 1e-6
ROPE_BASE = 10000.0
LANES = 128
HEAD_W = 2 * LANES
MIB = 1024 * 1024
LOG2E = math.log2(math.e)
NT_DIMS = (((1,), (1,)), ((), ()))
TN_DIMS = (((0,), (0,)), ((), ()))


class _Cfg:
    def __init__(self, d_model=4096, batch=2, seq=8192, depth=2, grid_w=64, ctx_len=256,
                 da_heads=6, mla_heads=12, mla_q_rank=768, mla_kv_rank=512,
                 cm_groups=8, peer_heads=8):
        self.D, self.B, self.S, self.L = d_model, batch, seq, depth
        self.grid_w, self.C = grid_w, ctx_len
        self.da_heads, self.da_dk, self.da_dv = da_heads, 128, 256
        self.mla_heads, self.q_rank, self.kv_rank = mla_heads, mla_q_rank, mla_kv_rank
        self.nope, self.rope, self.mla_dv = 128, 64, 128
        self.cm_groups, self.cm_chunk, self.cm_dg = cm_groups, 128, 128
        self.peer_heads, self.peer_keys, self.peer_dkey, self.topk = peer_heads, 128, 256, 16
        self.da_w = da_heads * self.da_dv
        self.mla_w = mla_heads * self.mla_dv
        self.cm_w = cm_groups * self.cm_dg
        self.mix_w = self.da_w + self.mla_w + self.cm_w
        self.off_da_k = da_heads * 2 * self.da_dk
        self.off_da_v = 2 * self.off_da_k
        self.off_cq = self.off_da_v + self.da_w
        self.off_ckv = self.off_cq + mla_q_rank
        self.off_kr = self.off_ckv + mla_kv_rank
        self.off_cm = self.off_kr + self.rope
        self.in_cols = self.off_cm + 2 * self.cm_w
        self.n_exp = self.peer_keys * self.peer_keys
        self.Tx = batch * seq
        self.Tc = batch * ctx_len
        self.T = self.Tx + self.Tc
        self.Sk = ctx_len + seq


def _pick(n, cands):
    for c in cands:
        if n % c == 0:
            return c
    raise ValueError(f"no tile in {cands} divides {n}")


def _params(sem, vmem_mib):
    return pltpu.CompilerParams(dimension_semantics=sem, vmem_limit_bytes=int(vmem_mib * MIB))


def _rms(x, g):
    return x * lax.rsqrt(jnp.mean(x * x, axis=-1, keepdims=True) + EPS) * g


def _gelu(x):
    return 0.5 * x * (1.0 + jnp.tanh(math.sqrt(2.0 / math.pi) * (x + 0.044715 * (x * x * x))))


def _mod_kernel(c_ref, w_ref, b_ref, o_ref):
    c = c_ref[...]
    s = (c * jax.nn.sigmoid(c)).astype(BF16)
    o_ref[...] = jnp.dot(s, w_ref[...].astype(BF16), preferred_element_type=F32) + b_ref[...]


def _mod_call(cfg, c_all, w_mod, b_mod):
    L, D = cfg.L, cfg.D
    n = 6 * D
    tn = _pick(n, (512, 256, 128))
    return pl.pallas_call(
        _mod_kernel,
        out_shape=jax.ShapeDtypeStruct((L, 8, n), F32),
        grid=(L, n // tn),
        in_specs=[pl.BlockSpec((8, D), lambda l, j: (0, 0)),
                  pl.BlockSpec((None, D, tn), lambda l, j: (l, 0, j)),
                  pl.BlockSpec((None, 1, tn), lambda l, j: (l, 0, j))],
        out_specs=pl.BlockSpec((None, 8, tn), lambda l, j: (l, 0, j)),
        compiler_params=_params(("arbitrary", "arbitrary"), 2 * D * tn * 4 / MIB + 3 * D * tn * 2 / MIB + 4),
        name="adaln_mod",
    )(c_all, w_mod, b_mod.reshape(L, 1, n))


def _cast_kernel(w_ref, o_ref):
    o_ref[...] = w_ref[...].astype(o_ref.dtype)


def _cast_call(w, l):
    _, R, Cw = w.shape
    tr = _pick(R, (512, 256, 128))
    return pl.pallas_call(
        _cast_kernel,
        out_shape=jax.ShapeDtypeStruct((R, Cw), BF16),
        grid=(R // tr,),
        in_specs=[pl.BlockSpec((None, tr, Cw), lambda i: (l, i, 0))],
        out_specs=pl.BlockSpec((tr, Cw), lambda i: (i, 0)),
        compiler_params=_params(("arbitrary",), 2 * tr * Cw * 6 / MIB + 4),
        name="weight_cast",
    )(w)


def _norm_mod_kernel(x_ref, g_ref, sh_ref, sc_ref, o_ref):
    y = _rms(x_ref[...], g_ref[...])
    o_ref[...] = (y * (1.0 + sc_ref[...]) + sh_ref[...]).astype(o_ref.dtype)


def _res_norm_mod_kernel(x_ref, r_ref, gt_ref, g_ref, sh_ref, sc_ref, xo_ref, o_ref):
    x = x_ref[...] + gt_ref[...] * r_ref[...]
    xo_ref[...] = x
    o_ref[...] = (_rms(x, g_ref[...]) * (1.0 + sc_ref[...]) + sh_ref[...]).astype(o_ref.dtype)


def _seg_of_tile(cfg, tm):
    return lambda i: jnp.minimum((i * tm) // cfg.S, cfg.B)


def _res_norm_mod_call(cfg, x, raw, gate_mods, k_gate, g, mods, k_shift, k_scale, rows):
    D = cfg.D
    tm = _pick(math.gcd(cfg.S, cfg.Tc), (256, 128))
    seg = _seg_of_tile(cfg, tm)
    row = pl.BlockSpec((tm, D), lambda i: (i, 0))
    vec = lambda k: pl.BlockSpec((None, None, 1, D), lambda i: (seg(i), k, 0, 0))
    return pl.pallas_call(
        _res_norm_mod_kernel,
        out_shape=(jax.ShapeDtypeStruct((rows, D), F32), jax.ShapeDtypeStruct((rows, D), BF16)),
        grid=(rows // tm,),
        in_specs=[row, row, vec(k_gate), pl.BlockSpec((1, D), lambda i: (0, 0)), vec(k_shift), vec(k_scale)],
        out_specs=(row, row),
        compiler_params=_params(("arbitrary",), 12 * tm * D * 4 / MIB + 4),
        name="residual_norm_modulate",
    )(x, raw, gate_mods, g.reshape(1, D), mods, mods)


def _norm_mod_call(cfg, x, g, mods, k_shift, k_scale, rows):
    D = cfg.D
    tm = _pick(math.gcd(cfg.S, cfg.Tc), (512, 256, 128))
    seg = _seg_of_tile(cfg, tm)
    return pl.pallas_call(
        _norm_mod_kernel,
        out_shape=jax.ShapeDtypeStruct((rows, D), BF16),
        grid=(rows // tm,),
        in_specs=[pl.BlockSpec((tm, D), lambda i: (i, 0)),
                  pl.BlockSpec((1, D), lambda i: (0, 0)),
                  pl.BlockSpec((None, None, 1, D), lambda i: (seg(i), k_shift, 0, 0)),
                  pl.BlockSpec((None, None, 1, D), lambda i: (seg(i), k_scale, 0, 0))],
        out_specs=pl.BlockSpec((tm, D), lambda i: (i, 0)),
        compiler_params=_params(("arbitrary",), 5 * tm * D * 4 / MIB + 4),
        name="norm_modulate",
    )(x, g.reshape(1, D), mods, mods)


def _mm_kernel(a_ref, b_ref, o_ref):
    o_ref[...] = jnp.dot(a_ref[...], b_ref[...], preferred_element_type=F32).astype(o_ref.dtype)


MM_VMEM_BUDGET_MIB = 48


def _mm_vmem_mib(tm, tn, K, osz):
    return (2 * (tm * K * 2 + K * tn * 2 + tm * tn * osz) + tm * tn * 4) / MIB + 4


def _mm_tiles(M, K, N, osz):
    fits = [(tm * tn / (tm + tn), tn, tm)
            for tm in (1536, 1024, 512, 256, 128) for tn in (N, 1536, 1024, 768, 512, 256, 128)
            if M % tm == 0 and N % tn == 0 and (tn == N or tn % LANES == 0)
            and _mm_vmem_mib(tm, tn, K, osz) <= MM_VMEM_BUDGET_MIB]
    if not fits:
        raise ValueError(f"no matmul tiling for {(M, K, N)}")
    _, tn, tm = max(fits)
    return tm, tn


def _mm_call(a, b, out_dtype, name, rows=None):
    M = a.shape[0] if rows is None else rows
    K, N = b.shape
    osz = jnp.dtype(out_dtype).itemsize
    tm, tn = _mm_tiles(M, K, N, osz)
    vm = _mm_vmem_mib(tm, tn, K, osz)
    return pl.pallas_call(
        _mm_kernel,
        out_shape=jax.ShapeDtypeStruct((M, N), out_dtype),
        grid=(M // tm, N // tn),
        in_specs=[pl.BlockSpec((tm, K), lambda i, j: (i, 0)),
                  pl.BlockSpec((K, tn), lambda i, j: (0, j))],
        out_specs=pl.BlockSpec((tm, tn), lambda i, j: (i, j)),
        compiler_params=_params(("arbitrary", "arbitrary"), vm),
        name=name,
    )(a, b)


def _mm_res_kernel(a_ref, b_ref, x_ref, g_ref, o_ref):
    acc = jnp.dot(a_ref[...], b_ref[...], preferred_element_type=F32)
    o_ref[...] = x_ref[...] + g_ref[...] * acc


def _mm_res_call(cfg, a, b, x, mods, k_gate, rows, name):
    K, N = b.shape
    tm = _pick(math.gcd(cfg.S, cfg.Tc), (512, 256, 128))
    tn = _pick(N, (1024, 512, 256, 128))
    seg = _seg_of_tile(cfg, tm)
    vm = 2 * (tm * K * 2 + K * tn * 2 + 2 * tm * tn * 4) / MIB + tm * tn * 4 / MIB + 4
    return pl.pallas_call(
        _mm_res_kernel,
        out_shape=jax.ShapeDtypeStruct((rows, N), F32),
        grid=(rows // tm, N // tn),
        in_specs=[pl.BlockSpec((tm, K), lambda i, j: (i, 0)),
                  pl.BlockSpec((K, tn), lambda i, j: (0, j)),
                  pl.BlockSpec((tm, tn), lambda i, j: (i, j)),
                  pl.BlockSpec((None, None, 1, tn), lambda i, j: (seg(i), k_gate, 0, j))],
        out_specs=pl.BlockSpec((tm, tn), lambda i, j: (i, j)),
        compiler_params=_params(("arbitrary", "arbitrary"), vm),
        name=name,
    )(a, b, x, mods)


def _key_block(cfg, tm):
    n_lat = cfg.Tx // tm
    per_b = cfg.S // tm
    per_c = cfg.C // tm

    def batch(i):
        return jnp.where(i < n_lat, i // per_b, (i - n_lat) // per_c)

    def row(i):
        return jnp.where(i < n_lat, per_c + i % per_b, (i - n_lat) % per_c)

    return batch, row


def _swap_halves(x, lane, half):
    first = (lane % (2 * half)) < half
    return jnp.where(first, pltpu.roll(x, LANES - half, 1), pltpu.roll(x, half, 1))


def _da_prep_kernel(q_ref, k_ref, v_ref, cos_ref, sin_ref, qo_ref, ko_ref, vo_ref, *, groups, qscale):
    cos = cos_ref[...]
    sin = sin_ref[...]
    lane = lax.broadcasted_iota(jnp.int32, cos.shape, 1)

    def rope(x):
        return x * cos + _swap_halves(x, lane, 32) * sin

    for g in range(groups):
        sl = slice(g * LANES, (g + 1) * LANES)
        qo_ref[:, sl] = (rope(q_ref[:, sl].astype(F32)) * qscale).astype(BF16)
        ko_ref[:, sl] = rope(k_ref[:, sl].astype(F32)).astype(BF16)
    vo_ref[...] = v_ref[...]


def _da_prep_call(cfg, z_da, cos, sin):
    W = cfg.da_w
    tm = min(cfg.C, HEAD_W)
    kb, kr = _key_block(cfg, tm)
    tok = lambda c: pl.BlockSpec((tm, W), lambda i: (i, c))
    tab = pl.BlockSpec((tm, LANES), lambda i: (i, 0))
    key = pl.BlockSpec((None, tm, W), lambda i: (kb(i), kr(i), 0))
    return pl.pallas_call(
        functools.partial(_da_prep_kernel, groups=W // LANES, qscale=cfg.da_dk ** -0.5 * LOG2E),
        out_shape=(jax.ShapeDtypeStruct((cfg.T, W), BF16),
                   jax.ShapeDtypeStruct((cfg.B, cfg.Sk, W), BF16),
                   jax.ShapeDtypeStruct((cfg.B, cfg.Sk, W), BF16)),
        grid=(cfg.T // tm,),
        in_specs=[tok(0), tok(1), tok(2), tab, tab],
        out_specs=(tok(0), key, key),
        compiler_params=_params(("arbitrary",), 12 * tm * W * 2 / MIB + 8),
        name="da_prep",
    )(z_da, z_da, z_da, cos, sin)


def _mla_prep_kernel(z_ref, gq_ref, gkv_ref, wq_ref, wkv_ref, cos_ref, sin_ref,
                     qo_ref, ko_ref, vo_ref, *, heads, q_rank, kv_rank, qscale):
    z = z_ref[...].astype(F32)
    cos = cos_ref[...]
    sin = sin_ref[...]
    lane = lax.broadcasted_iota(jnp.int32, cos.shape, 1)

    def rope(x):
        return x * cos + _swap_halves(x, lane, 16) * sin

    cq = _rms(z[:, :q_rank], gq_ref[...]).astype(BF16)
    ckv = _rms(z[:, q_rank:q_rank + kv_rank], gkv_ref[...]).astype(BF16)
    kpe = rope(z[:, q_rank + kv_rank:q_rank + kv_rank + LANES]).astype(BF16)
    q = jnp.dot(cq, wq_ref[...], preferred_element_type=F32)
    kv = jnp.dot(ckv, wkv_ref[...], preferred_element_type=F32)
    for h in range(heads):
        a, b, c = 2 * h * LANES, (2 * h + 1) * LANES, (2 * h + 2) * LANES
        qo_ref[:, a:b] = (q[:, a:b] * qscale).astype(BF16)
        qo_ref[:, b:c] = (rope(q[:, b:c]) * qscale).astype(BF16)
        ko_ref[:, a:b] = kv[:, a:b].astype(BF16)
        ko_ref[:, b:c] = kpe
        vo_ref[:, a:b] = kv[:, b:c].astype(BF16)
        vo_ref[:, b:c] = jnp.ones((z.shape[0], LANES), BF16)


def _mla_prep_call(cfg, z_mla, gq, gkv, wq, wkv, cos, sin):
    H = cfg.mla_heads
    Wz = z_mla.shape[1]
    tm = min(cfg.C, HEAD_W)
    kb, kr = _key_block(cfg, tm)
    full = lambda a: pl.BlockSpec(a.shape, lambda i: (0,) * a.ndim)
    tab = pl.BlockSpec((tm, LANES), lambda i: (i, 0))
    return pl.pallas_call(
        functools.partial(_mla_prep_kernel, heads=H, q_rank=cfg.q_rank, kv_rank=cfg.kv_rank,
                          qscale=(cfg.nope + cfg.rope) ** -0.5 * LOG2E),
        out_shape=(jax.ShapeDtypeStruct((cfg.T, H * HEAD_W), BF16),
                   jax.ShapeDtypeStruct((cfg.B, cfg.Sk, H * HEAD_W), BF16),
                   jax.ShapeDtypeStruct((cfg.B, cfg.Sk, H * HEAD_W), BF16)),
        grid=(cfg.T // tm,),
        in_specs=[pl.BlockSpec((tm, Wz), lambda i: (i, 0)), full(gq), full(gkv), full(wq), full(wkv), tab, tab],
        out_specs=(pl.BlockSpec((tm, H * HEAD_W), lambda i: (i, 0)),
                   pl.BlockSpec((None, tm, H * HEAD_W), lambda i: (kb(i), kr(i), 0)),
                   pl.BlockSpec((None, tm, H * HEAD_W), lambda i: (kb(i), kr(i), 0))),
        compiler_params=_params(("arbitrary",), 2 * (wq.size + wkv.size) * 2 / MIB + 24 * tm * H * HEAD_W / MIB + 8),
        name="mla_prep",
    )(z_mla, gq, gkv, wq, wkv, cos, sin)


ATTN_SUB_ROWS = 512


def _online_step(q, k, v, m, l, acc):
    s = lax.dot_general(q, k, NT_DIMS, preferred_element_type=F32)
    m_new = jnp.maximum(m, jnp.max(s, axis=-1, keepdims=True))
    alpha = jnp.exp2(m - m_new)
    p = jnp.exp2(s - m_new)
    l = alpha * l + jnp.sum(p, axis=-1, keepdims=True)
    acc = alpha * acc + jnp.dot(p.astype(BF16), v, preferred_element_type=F32)
    return m_new, l, acc


def _da_attn_kernel(lam_ref, g_ref, q_ref, k_ref, v_ref, o_ref, *, tk, nk, sub, lam_init):
    lv = lam_ref[...]
    lam = (jnp.exp(jnp.sum(lv[0:1] * lv[1:2], axis=-1, keepdims=True))
           - jnp.exp(jnp.sum(lv[2:3] * lv[3:4], axis=-1, keepdims=True)) + lam_init)
    neg = jnp.full((sub, 1), -jnp.inf, F32)
    zero = jnp.zeros((sub, 1), F32)
    acc0 = jnp.zeros((sub, v_ref.shape[1]), F32)
    for q0 in range(0, q_ref.shape[0], sub):
        q1 = q_ref[q0:q0 + sub, :LANES]
        q2 = q_ref[q0:q0 + sub, LANES:]
        m1, l1, a1, m2, l2, a2 = neg, zero, acc0, neg, zero, acc0
        for kk in range(nk):
            rows = slice(kk * tk, (kk + 1) * tk)
            v = v_ref[rows, :]
            m1, l1, a1 = _online_step(q1, k_ref[rows, :LANES], v, m1, l1, a1)
            m2, l2, a2 = _online_step(q2, k_ref[rows, LANES:], v, m2, l2, a2)
        o = a1 / l1 - lam * (a2 / l2)
        o_ref[q0:q0 + sub, :] = (_rms(o, g_ref[...]) * (1.0 - lam_init)).astype(o_ref.dtype)


def _into_mix(kernel, n_in, mix, mix_shape):
    out_shape = jax.ShapeDtypeStruct(mix_shape, BF16)

    def with_unused_mix_ref(*refs):
        return kernel(*refs[:n_in], *refs[n_in + 1:])

    return with_unused_mix_ref, [pl.BlockSpec(memory_space=pl.ANY)], (mix,), {n_in: 0}, out_shape


def _da_attn_call(cfg, q, k, v, lam_vec, subln_g, lam_init, mix, mix_shape, *, q_row0, sq, sk):
    H, dv = cfg.da_heads, cfg.da_dv
    tq = _pick(sq, (512, 256, 128))
    sub = min(tq, ATTN_SUB_ROWS)
    tk = _pick(sk, (768, 512, 256, 128))
    nq = sq // tq
    q0 = q_row0 // tq
    vm = 2 * (2 * sk * HEAD_W * 2) / MIB + 16 * sub * tk * 4 / MIB + 12
    kern, mix_spec, mix_arg, alias, out_shape = _into_mix(
        functools.partial(_da_attn_kernel, tk=tk, nk=sk // tk, sub=sub, lam_init=lam_init), 5, mix, mix_shape)
    return pl.pallas_call(
        kern,
        out_shape=out_shape,
        input_output_aliases=alias,
        grid=(cfg.B, H, nq),
        in_specs=[pl.BlockSpec((4, LANES), lambda b, h, i: (0, 0)),
                  pl.BlockSpec((1, dv), lambda b, h, i: (0, 0)),
                  pl.BlockSpec((tq, HEAD_W), lambda b, h, i: (q0 + b * nq + i, h)),
                  pl.BlockSpec((None, sk, HEAD_W), lambda b, h, i: (b, 0, h)),
                  pl.BlockSpec((None, sk, dv), lambda b, h, i: (b, 0, h))] + mix_spec,
        out_specs=pl.BlockSpec((tq, dv), lambda b, h, i: (q0 + b * nq + i, h)),
        compiler_params=_params(("arbitrary", "arbitrary", "arbitrary"), vm),
        name="diff_attention",
    )(lam_vec, subln_g.reshape(1, dv), q, k, v, *mix_arg)


def _mla_attn_kernel(q_ref, k_ref, v_ref, o_ref, *, tk, nk, sub):
    dv = o_ref.shape[1]
    for q0 in range(0, q_ref.shape[0], sub):
        q = q_ref[q0:q0 + sub, :]
        m, acc = jnp.full((sub, 1), -jnp.inf, F32), jnp.zeros((sub, v_ref.shape[1]), F32)
        for kk in range(nk):
            rows = slice(kk * tk, (kk + 1) * tk)
            s = lax.dot_general(q, k_ref[rows, :], NT_DIMS, preferred_element_type=F32)
            m_new = jnp.maximum(m, jnp.max(s, axis=-1, keepdims=True))
            p = jnp.exp2(s - m_new).astype(BF16)
            acc = jnp.exp2(m - m_new) * acc + jnp.dot(p, v_ref[rows, :], preferred_element_type=F32)
            m = m_new
        o_ref[q0:q0 + sub, :] = (acc[:, :dv] / acc[:, dv:dv + 1]).astype(o_ref.dtype)


def _mla_attn_call(cfg, q, k, v, mix, mix_shape, *, q_row0, sq, sk):
    H, dv = cfg.mla_heads, cfg.mla_dv
    col0 = cfg.da_w // dv
    tq = _pick(sq, (1024, 512, 256, 128))
    sub = min(tq, ATTN_SUB_ROWS)
    tk = _pick(sk, (768, 512, 256, 128))
    nq = sq // tq
    q0 = q_row0 // tq
    vm = 2 * (sk * 2 * HEAD_W * 2) / MIB + 6 * sub * tk * 4 / MIB + 16
    kern, mix_spec, mix_arg, alias, out_shape = _into_mix(
        functools.partial(_mla_attn_kernel, tk=tk, nk=sk // tk, sub=sub), 3, mix, mix_shape)
    return pl.pallas_call(
        kern,
        out_shape=out_shape,
        input_output_aliases=alias,
        grid=(cfg.B, H, nq),
        in_specs=[pl.BlockSpec((tq, HEAD_W), lambda b, h, i: (q0 + b * nq + i, h)),
                  pl.BlockSpec((None, sk, HEAD_W), lambda b, h, i: (b, 0, h)),
                  pl.BlockSpec((None, sk, 2 * dv), lambda b, h, i: (b, 0, h))] + mix_spec,
        out_specs=pl.BlockSpec((tq, dv), lambda b, h, i: (q0 + b * nq + i, col0 + h)),
        compiler_params=_params(("arbitrary", "arbitrary", "arbitrary"), vm),
        name="mla_attention",
    )(q, k, v, *mix_arg)


def _chunk_mlp_kernel(z_ref, g_ref, ws_ref, b_ref, o_ref, *, groups, chunk):
    uv = _gelu(z_ref[...].astype(F32))
    w = groups * LANES
    u = uv[:, :w]
    v = _rms(uv[:, w:], g_ref[...]).astype(BF16)
    for c in range(z_ref.shape[0] // chunk):
        rows = slice(c * chunk, (c + 1) * chunk)
        for g in range(groups):
            cols = slice(g * LANES, (g + 1) * LANES)
            y = jnp.dot(ws_ref[g], v[rows, cols], preferred_element_type=F32) + b_ref[:, cols]
            o_ref[rows, cols] = (u[rows, cols] * y).astype(o_ref.dtype)


def _chunk_mlp_call(cfg, z_cm, g, ws, b_full, mix, mix_shape, rows):
    W = cfg.cm_w
    col0 = (cfg.da_w + cfg.mla_w) // W
    tm = _pick(math.gcd(cfg.S, cfg.Tc), (256, 128))
    kern, mix_spec, mix_arg, alias, out_shape = _into_mix(
        functools.partial(_chunk_mlp_kernel, groups=cfg.cm_groups, chunk=cfg.cm_chunk), 4, mix, mix_shape)
    return pl.pallas_call(
        kern,
        out_shape=out_shape,
        input_output_aliases=alias,
        grid=(rows // tm,),
        in_specs=[pl.BlockSpec((tm, 2 * W), lambda i: (i, 0)),
                  pl.BlockSpec((1, W), lambda i: (0, 0)),
                  pl.BlockSpec(ws.shape, lambda i: (0, 0, 0)),
                  pl.BlockSpec(b_full.shape, lambda i: (0, 0))] + mix_spec,
        out_specs=pl.BlockSpec((tm, W), lambda i: (i, col0)),
        compiler_params=_params(("arbitrary",), 40 * tm * W / MIB + 8),
        name="chunk_mlp",
    )(z_cm, g.reshape(1, W), ws, b_full, *mix_arg)


INT32_MIN = -2 ** 31


def _order_key(x):
    b = lax.bitcast_convert_type(x, jnp.int32)
    return jnp.where(b < 0, b ^ 0x7FFFFFFF, b)


def _order_key_inv(k):
    return lax.bitcast_convert_type(jnp.where(k < 0, k ^ 0x7FFFFFFF, k), F32)


def _top_k_sublanes(keys, pos, topk):
    iota_k = lax.broadcasted_iota(jnp.int32, (topk, keys[0].shape[1]), 0)

    def body(k, carry):
        out = []
        for (work, vals), p in zip(carry, pos):
            m = jnp.max(work, axis=0, keepdims=True)
            idx = jnp.min(jnp.where(work == m, p, jnp.inf), axis=0, keepdims=True)
            out.append((jnp.where(p == idx, INT32_MIN + k, work), jnp.where(iota_k == k, m, vals)))
        return tuple(out)

    init = tuple((w, jnp.zeros((topk, w.shape[1]), jnp.int32)) for w in keys)
    res = lax.fori_loop(0, topk, body, init)
    return [(jnp.where(w < INT32_MIN + topk, w - INT32_MIN, topk), v) for w, v in res]


def _pair_blocks(topk):
    return [(0, topk)] + [(a, 8) for a in range(1, 8)]


def _route_kernel(q_ref, keys_ref, er_ref, cnt_ref, ec_ref, rk_ref, *, topk):
    half = q_ref.shape[1] // 2
    tm = q_ref.shape[0]
    s_row = lax.dot_general(keys_ref[0], q_ref[:, :half], NT_DIMS, preferred_element_type=F32)
    s_col = lax.dot_general(keys_ref[1], q_ref[:, half:], NT_DIMS, preferred_element_type=F32)
    iota = lax.broadcasted_iota(jnp.int32, (s_row.shape[0], LANES), 0).astype(F32)
    groups = [slice(g * LANES, (g + 1) * LANES) for g in range(tm // LANES)]
    res = _top_k_sublanes([_order_key(s[:, lanes]) for lanes in groups for s in (s_row, s_col)],
                          [iota] * (2 * len(groups)), topk)
    rank_r = jnp.concatenate([r for r, _ in res[0::2]], axis=1)
    rank_c = jnp.concatenate([r for r, _ in res[1::2]], axis=1)
    top_r = jnp.concatenate([_order_key_inv(v) for _, v in res[0::2]], axis=1)
    top_c = jnp.concatenate([_order_key_inv(v) for _, v in res[1::2]], axis=1)

    blocks = _pair_blocks(topk)
    cand_parts, pos_parts, valid_parts = [], [], []
    for a, nb in blocks:
        b_iota = lax.broadcasted_iota(jnp.int32, (nb, tm), 0)
        cand_parts.append(top_r[a:a + 1, :] + top_c[:nb, :])
        pos_parts.append((a * topk + b_iota).astype(F32))
        valid_parts.append((a + 1) * (b_iota + 1) <= topk)
    a_iota = lax.broadcasted_iota(jnp.int32, (topk - 8, tm), 0) + 8
    cand_parts.append(top_r[8:, :] + top_c[0:1, :])
    pos_parts.append((a_iota * topk).astype(F32))
    valid_parts.append(a_iota < topk)
    cand = jnp.concatenate(cand_parts, axis=0)
    pos = jnp.concatenate(pos_parts, axis=0)
    valid = jnp.concatenate(valid_parts, axis=0)
    cand_key = jnp.where(valid, _order_key(cand), INT32_MIN + topk)
    (rank_p, best), = _top_k_sublanes([cand_key], [pos], topk)
    chosen = rank_p < topk
    best0 = _order_key_inv(best[0:1, :])
    z = jnp.sum(jnp.where(chosen, jnp.exp(cand - best0), 0.0), axis=0, keepdims=True)
    chosen_f = chosen.astype(F32)
    cnt = jnp.zeros(s_row.shape, F32)
    row0 = 0
    for a, nb in blocks:
        cnt_a = jnp.sum(chosen_f[row0:row0 + nb, :], axis=0, keepdims=True)
        cnt = jnp.where(rank_r == a, cnt_a, cnt)
        row0 += nb
    for a in range(8, topk):
        cnt = jnp.where(rank_r == a, chosen_f[row0 + a - 8:row0 + a - 7, :], cnt)
    er_ref[...] = jnp.where(rank_r < topk, jnp.exp(s_row - top_r[0:1, :]), 0.0)
    cnt_ref[...] = cnt
    ec_ref[...] = jnp.where(rank_c < topk, jnp.exp(s_col - top_c[0:1, :]) / z, 0.0)
    rk_ref[...] = rank_c.astype(F32)


def _route_call(cfg, qp, keys, rows):
    H, nk = cfg.peer_heads, cfg.peer_keys
    tm = _pick(rows, (256, 128))
    out = lambda dt: jax.ShapeDtypeStruct((H, nk, rows), dt)
    ospec = pl.BlockSpec((None, nk, tm), lambda i, h: (h, 0, i))
    return pl.pallas_call(
        functools.partial(_route_kernel, topk=cfg.topk),
        out_shape=(out(F32), out(F32), out(F32), out(F32)),
        grid=(rows // tm, H),
        in_specs=[pl.BlockSpec((tm, cfg.peer_dkey), lambda i, h: (i, h)),
                  pl.BlockSpec((None, 2, nk, cfg.peer_dkey // 2), lambda i, h: (h, 0, 0, 0))],
        out_specs=(ospec, ospec, ospec, ospec),
        compiler_params=_params(("arbitrary", "arbitrary"), 32),
        name="peer_route",
    )(qp, keys)


def _peer_act_kernel(u_ref, h_ref, o_ref):
    o_ref[...] = _gelu(lax.dot_general(u_ref[...], h_ref[...], NT_DIMS, preferred_element_type=F32)).astype(o_ref.dtype)


def _peer_act_call(cfg, u, h2, rows):
    D = cfg.D
    tn, tm = _mm_tiles(cfg.n_exp, D, rows, 2)
    return pl.pallas_call(
        _peer_act_kernel,
        out_shape=jax.ShapeDtypeStruct((cfg.n_exp, rows), BF16),
        grid=(rows // tm, cfg.n_exp // tn),
        in_specs=[pl.BlockSpec((tn, D), lambda i, n: (n, 0)),
                  pl.BlockSpec((tm, D), lambda i, n: (i, 0))],
        out_specs=pl.BlockSpec((tn, tm), lambda i, n: (n, i)),
        compiler_params=_params(("arbitrary", "arbitrary"), _mm_vmem_mib(tn, tm, D, 2) + 4 * tn * tm * 4 / MIB),
        name="peer_activations",
    )(u, h2)


def _peer_out_kernel(gt_ref, er_ref, cnt_ref, ec_ref, rk_ref, v_ref, o_ref, *, heads):
    @pl.when(pl.program_id(1) == 0)
    def _():
        o_ref[...] = jnp.zeros_like(o_ref)

    tk, tm = gt_ref.shape
    R = tk // LANES
    w = jnp.zeros((R, LANES, tm), F32)
    for h in range(heads):
        hit = rk_ref[h][None, :, :] < cnt_ref[h][:, None, :]
        w = w + jnp.where(hit, ec_ref[h][None, :, :], 0.0) * er_ref[h][:, None, :]
    pt = (w.reshape(tk, tm) * gt_ref[...].astype(F32)).astype(BF16)
    o_ref[...] += lax.dot_general(pt, v_ref[...], TN_DIMS, preferred_element_type=F32)


def _peer_out_call(cfg, gt, er, cnt, ec, rk, v, rows):
    H, nk, D = cfg.peer_heads, cfg.peer_keys, cfg.D
    tm = _pick(rows, (512, 256, 128))
    tk = 1024
    R = tk // LANES
    row_spec = pl.BlockSpec((H, R, tm), lambda i, k: (0, k, i))
    col_spec = pl.BlockSpec((H, nk, tm), lambda i, k: (0, 0, i))
    vm = 2 * (tk * tm * 2 + tk * D * 2 + tm * D * 4 + 2 * H * nk * tm * 4) / MIB + 6 * tk * tm * 4 / MIB + 4
    return pl.pallas_call(
        functools.partial(_peer_out_kernel, heads=H),
        out_shape=jax.ShapeDtypeStruct((rows, D), F32),
        grid=(rows // tm, cfg.n_exp // tk),
        in_specs=[pl.BlockSpec((tk, tm), lambda i, k: (k, i)), row_spec, row_spec, col_spec, col_spec,
                  pl.BlockSpec((tk, D), lambda i, k: (k, 0))],
        out_specs=pl.BlockSpec((tm, D), lambda i, k: (i, 0)),
        compiler_params=_params(("arbitrary", "arbitrary"), vm),
        name="peer_output",
    )(gt, er, cnt, ec, rk, v)


def _final_norm_kernel(x_ref, r_ref, gt_ref, g_ref, o_ref):
    o_ref[...] = _rms(x_ref[...] + gt_ref[...] * r_ref[...], g_ref[...])


def _final_norm_call(cfg, x, raw, gate_mods, k_gate, g):
    D = cfg.D
    tm = _pick(math.gcd(cfg.S, cfg.Tc), (256, 128))
    seg = _seg_of_tile(cfg, tm)
    return pl.pallas_call(
        _final_norm_kernel,
        out_shape=jax.ShapeDtypeStruct((cfg.Tx, D), F32),
        grid=(cfg.Tx // tm,),
        in_specs=[pl.BlockSpec((tm, D), lambda i: (i, 0)), pl.BlockSpec((tm, D), lambda i: (i, 0)),
                  pl.BlockSpec((None, None, 1, D), lambda i: (seg(i), k_gate, 0, 0)),
                  pl.BlockSpec((1, D), lambda i: (0, 0))],
        out_specs=pl.BlockSpec((tm, D), lambda i: (i, 0)),
        compiler_params=_params(("arbitrary",), 8 * tm * D * 4 / MIB + 4),
        name="final_norm",
    )(x, raw, gate_mods, g.reshape(1, D))


def _rope_tables(cfg, half, pad):
    pos = jnp.arange(cfg.S, dtype=jnp.int32)
    inv = ROPE_BASE ** (-jnp.arange(half, dtype=F32) / half)
    ang_r = (pos // cfg.grid_w).astype(F32)[:, None] * inv[None, :]
    ang_c = (pos % cfg.grid_w).astype(F32)[:, None] * inv[None, :]
    cos = jnp.concatenate([jnp.cos(ang_r)] * 2 + [jnp.cos(ang_c)] * 2, axis=-1)
    sin = jnp.concatenate([-jnp.sin(ang_r), jnp.sin(ang_r), -jnp.sin(ang_c), jnp.sin(ang_c)], axis=-1)
    cos = jnp.concatenate([jnp.tile(cos, (cfg.B, 1)), jnp.ones((cfg.Tc, 4 * half), F32)], axis=0)
    sin = jnp.concatenate([jnp.tile(sin, (cfg.B, 1)), jnp.zeros((cfg.Tc, 4 * half), F32)], axis=0)
    if pad:
        cos = jnp.pad(cos, ((0, 0), (0, pad)))
        sin = jnp.pad(sin, ((0, 0), (0, pad)))
    return cos, sin


def _forward(cfg, x, c, ctx, c_ctx, w_mod, b_mod, norm1_g, norm2_g, w_in, da_lambda, da_subln_g,
             mla_q_norm_g, mla_kv_norm_g, mla_w_uq, mla_w_ukv, cm_v_norm_g, cm_w_s, cm_b_s,
             w_out, peer_w_q, peer_sub_keys, peer_u, peer_v, final_norm_g):
    B, S, C, D, L = cfg.B, cfg.S, cfg.C, cfg.D, cfg.L
    assert cfg.da_dk == LANES and cfg.nope == LANES and cfg.peer_keys == LANES and cfg.topk == 16
    assert C % LANES == 0 and S % C == 0 and cfg.q_rank % LANES == 0 and cfg.kv_rank % LANES == 0
    assert (cfg.da_w + cfg.mla_w) % cfg.cm_w == 0

    xs = jnp.concatenate([x.reshape(B * S, D), ctx.reshape(B * C, D)], axis=0)
    c_all = jnp.concatenate([c, c_ctx[None, :], jnp.zeros((8 - B - 1, D), F32)], axis=0)
    mods_all = _mod_call(cfg, c_all, w_mod, b_mod).reshape(L, 8, 6, 1, D)
    cos_da, sin_da = _rope_tables(cfg, 32, 0)
    cos_ml, sin_ml = _rope_tables(cfg, 16, LANES - cfg.rope)

    for l in range(L):
        last = l == L - 1
        rows = cfg.Tx if last else cfg.T
        mods = mods_all[l]
        lam_init = 0.8 - 0.6 * math.exp(-0.3 * l)

        w_in_l = w_in[l]
        w_da = w_in_l[:, :cfg.off_cq].astype(BF16)
        ml_cols = cfg.off_cm - cfg.off_cq
        w_ml = jnp.pad(w_in_l[:, cfg.off_cq:cfg.off_cm], ((0, 0), (0, -ml_cols % HEAD_W))).astype(BF16)
        w_cm = w_in_l[:, cfg.off_cm:].astype(BF16)
        H = cfg.mla_heads
        wq = jnp.pad(mla_w_uq[l].reshape(cfg.q_rank, H, cfg.nope + cfg.rope),
                     ((0, 0), (0, 0), (0, LANES - cfg.rope))).reshape(cfg.q_rank, H * HEAD_W).astype(BF16)
        wkv = mla_w_ukv[l].astype(BF16)
        b_full = jnp.repeat(cm_b_s[l].T, cfg.cm_dg, axis=1)
        keys = peer_sub_keys[l].astype(BF16)

        if l == 0:
            h1 = _norm_mod_call(cfg, xs, norm1_g[l], mods, 0, 1, cfg.T)
        else:
            xs, h1 = _res_norm_mod_call(cfg, xs, peer_raw, mods_all[l - 1], 5, norm1_g[l], mods, 0, 1, cfg.T)
        z_da = _mm_call(h1, w_da, BF16, "in_proj_da")
        z_ml = _mm_call(h1, w_ml, BF16, "in_proj_mla")
        z_cm = _mm_call(h1, w_cm, BF16, "in_proj_cm", rows=rows)
        q_da, k_da, v_da = _da_prep_call(cfg, z_da, cos_da, sin_da)
        q_ml, k_ml, v_ml = _mla_prep_call(cfg, z_ml, mla_q_norm_g[l].reshape(1, -1), mla_kv_norm_g[l].reshape(1, -1),
                                          wq, wkv, cos_ml, sin_ml)
        mshape = (rows, cfg.mix_w)
        da_args = (cfg, q_da, k_da, v_da, da_lambda[l], da_subln_g[l], lam_init)
        mix = _da_attn_call(*da_args, jnp.zeros(mshape, BF16), mshape, q_row0=0, sq=S, sk=cfg.Sk)
        mix = _mla_attn_call(cfg, q_ml, k_ml, v_ml, mix, mshape, q_row0=0, sq=S, sk=cfg.Sk)
        if not last:
            mix = _da_attn_call(*da_args, mix, mshape, q_row0=cfg.Tx, sq=C, sk=C)
            mix = _mla_attn_call(cfg, q_ml, k_ml, v_ml, mix, mshape, q_row0=cfg.Tx, sq=C, sk=C)
        mix = _chunk_mlp_call(cfg, z_cm, cm_v_norm_g[l], cm_w_s[l].astype(BF16), b_full, mix, mshape, rows)
        xs = _mm_res_call(cfg, mix, _cast_call(w_out, l), xs, mods, 2, rows, "out_proj")

        h2 = _norm_mod_call(cfg, xs, norm2_g[l], mods, 3, 4, rows)
        qp = _mm_call(h2, _cast_call(peer_w_q, l), BF16, "peer_query")
        er, cnt, ec, rk = _route_call(cfg, qp, keys, rows)
        gt = _peer_act_call(cfg, _cast_call(peer_u, l), h2, rows)
        peer_raw = _peer_out_call(cfg, gt, er, cnt, ec, rk, _cast_call(peer_v, l), rows)

    return _final_norm_call(cfg, xs, peer_raw, mods_all[L - 1], 5, final_norm_g).reshape(B, S, D)


def kernel(x, c, ctx, c_ctx, w_mod, b_mod, norm1_g, norm2_g, w_in, da_lambda, da_subln_g, mla_q_norm_g,
           mla_kv_norm_g, mla_w_uq, mla_w_ukv, cm_v_norm_g, cm_w_s, cm_b_s, w_out, peer_w_q, peer_sub_keys,
           peer_u, peer_v, final_norm_g):
    return _forward(_Cfg(), x, c, ctx, c_ctx, w_mod, b_mod, norm1_g, norm2_g, w_in, da_lambda, da_subln_g,
                    mla_q_norm_g, mla_kv_norm_g, mla_w_uq, mla_w_ukv, cm_v_norm_g, cm_w_s, cm_b_s, w_out,
                    peer_w_q, peer_sub_keys, peer_u, peer_v, final_norm_g)
```
